```python
import math
import jax, jax.numpy as jnp
from jax import lax
import numpy as np

D_MODEL = 1024
BATCH = 16
SEQ = 4096
DEPTH = 4

CHUNK = 64
Q_BLOCK = 128
EPS = 1e-6

GLA_WIDTH = D_MODEL // 2
GLA_HEADS = 4
GLA_DK = GLA_WIDTH // 2 // GLA_HEADS
GLA_DV = GLA_WIDTH // GLA_HEADS
GLA_RANK = 16
GLA_TAU = 16.0

DIFF_WIDTH = D_MODEL - GLA_WIDTH
DIFF_HEADS = 4
DIFF_DH = DIFF_WIDTH // DIFF_HEADS // 2
DIFF_DV = 2 * DIFF_DH

D_FF = ((8 * D_MODEL // 3 + 127) // 128) * 128
CONV_W = 3

IN_SPLITS = (GLA_HEADS * GLA_DK, GLA_HEADS * GLA_DK, GLA_WIDTH, GLA_WIDTH, GLA_RANK,
             2 * DIFF_HEADS * DIFF_DH, 2 * DIFF_HEADS * DIFF_DH, DIFF_HEADS * DIFF_DV)
IN_COLS = sum(IN_SPLITS)

kernel_name = "hybrid_gla_diffattn_convffn_chunk_causal"


def rms_norm(x, g):
    xf = x.astype(jnp.float32)
    y = xf * lax.rsqrt(jnp.mean(xf * xf, axis=-1, keepdims=True) + EPS)
    return (y * g.astype(jnp.float32)).astype(x.dtype)


def split_cols(p):
    offs, acc = [], 0
    for s in IN_SPLITS[:-1]:
        acc += s
        offs.append(acc)
    return jnp.split(p, offs, axis=-1)


def gla_mixer(q, k, v, r, gate_low, w_gk_up, b_gk, norm_g):
    B, S, _ = q.shape
    N = S // CHUNK
    f32 = jnp.float32
    log_a = jax.nn.log_sigmoid((gate_low @ w_gk_up + b_gk).astype(f32)) / GLA_TAU

    def heads(t, d):
        return t.reshape(B, N, CHUNK, GLA_HEADS, d).transpose(0, 3, 1, 2, 4).astype(f32)

    qh = heads(q, GLA_DK) * (GLA_DK ** -0.5)
    kh = heads(k, GLA_DK)
    vh = heads(v, GLA_DV)
    la = heads(log_a, GLA_DK)
    cum = jnp.cumsum(la, axis=3)
    total = cum[:, :, :, -1:, :]
    k_dec = kh * jnp.exp(total - cum)
    q_dec = qh * jnp.exp(total)
    a_intra = jnp.einsum('bhncd,bhnsd->bhncs', qh, k_dec)
    o_intra = jnp.einsum('bhncs,bhnse->bhnce', a_intra, vh)
    chunk_kv = jnp.einsum('bhnsd,bhnse->bhnde', k_dec, vh)
    decay = jnp.exp(total[:, :, :, 0, :])

    def step(state, inp):
        d, kv = inp
        return d[..., None] * state + kv, state

    init = jnp.zeros((B, GLA_HEADS, GLA_DK, GLA_DV), f32)
    _, s_prev = lax.scan(step, init, (jnp.moveaxis(decay, 2, 0), jnp.moveaxis(chunk_kv, 2, 0)))
    s_prev = jnp.moveaxis(s_prev, 0, 2)
    o_inter = jnp.einsum('bhncd,bhnde->bhnce', q_dec, s_prev)
    o = (o_intra + o_inter).transpose(0, 2, 3, 1, 4).reshape(B, S, GLA_HEADS, GLA_DV)
    o = rms_norm(o, norm_g).reshape(B, S, GLA_WIDTH)
    return (o * jax.nn.silu(r.astype(f32))).astype(q.dtype)


def diff_mixer(q, k, v, q_norm_g, k_norm_g, lam_q1, lam_k1, lam_q2, lam_k2, sub_g, lambda_init):
    B, S, _ = q.shape
    f32 = jnp.float32
    NB = S // Q_BLOCK
    qh = rms_norm(q.reshape(B, S, DIFF_HEADS, 2, DIFF_DH).astype(f32), q_norm_g) * (DIFF_DH ** -0.5)
    kh = rms_norm(k.reshape(B, S, DIFF_HEADS, 2, DIFF_DH).astype(f32), k_norm_g)
    qh = qh.transpose(0, 2, 3, 1, 4)
    kh = kh.transpose(0, 2, 3, 1, 4)
    vh = v.reshape(B, S, DIFF_HEADS, DIFF_DV).transpose(0, 2, 1, 3).astype(f32)
    lam = (jnp.exp(jnp.sum(lam_q1.astype(f32) * lam_k1.astype(f32)))
           - jnp.exp(jnp.sum(lam_q2.astype(f32) * lam_k2.astype(f32))) + lambda_init)
    qb = qh.reshape(B, DIFF_HEADS, 2, NB, Q_BLOCK, DIFF_DH).transpose(3, 0, 1, 2, 4, 5)
    key_chunk = jnp.arange(S) // CHUNK

    def block(args):
        q_blk, i = args
        s = jnp.einsum('bhjqd,bhjkd->bhjqk', q_blk, kh)
        q_chunk = (i * Q_BLOCK + jnp.arange(Q_BLOCK)) // CHUNK
        mask = key_chunk[None, :] <= q_chunk[:, None]
        p = jax.nn.softmax(jnp.where(mask, s, -jnp.inf), axis=-1)
        a = p[:, :, 0] - lam * p[:, :, 1]
        return jnp.einsum('bhqk,bhke->bhqe', a, vh)

    o = lax.map(block, (qb, jnp.arange(NB)))
    o = o.transpose(1, 0, 3, 2, 4).reshape(B, S, DIFF_HEADS, DIFF_DV)
    o = rms_norm(o, sub_g) * (1.0 - lambda_init)
    return o.reshape(B, S, DIFF_WIDTH).astype(q.dtype)


def conv_ffn(h, w_up, conv_w, conv_b, w_down):
    u = h @ w_up
    C = u.shape[-1]
    u = lax.conv_general_dilated(u, conv_w[:, None, :], window_strides=(1,),
                                 padding=[(CONV_W - 1, 0)],
                                 dimension_numbers=('NWC', 'WIO', 'NWC'),
                                 feature_group_count=C) + conv_b
    gate, val = jnp.split(u, 2, axis=-1)
    return (jax.nn.silu(gate) * val) @ w_down


def setup_inputs(seed: int = 0) -> dict:
    key = jax.random.key(seed)
    ks = jax.random.split(key, 20)
    f32 = jnp.float32
    nrm = lambda k, shape, s: jax.random.normal(k, shape, f32) * s
    res_scale = (2 * DEPTH) ** -0.5
    return {
        "x": jax.random.normal(ks[0], (BATCH, SEQ, D_MODEL), f32),
        "norm1_g": 1.0 + nrm(ks[1], (DEPTH, D_MODEL), 0.02),
        "w_in": nrm(ks[2], (DEPTH, D_MODEL, IN_COLS), D_MODEL ** -0.5),
        "gla_w_gk_up": nrm(ks[3], (DEPTH, GLA_RANK, GLA_HEADS * GLA_DK), GLA_RANK ** -0.5),
        "gla_b_gk": nrm(ks[4], (DEPTH, GLA_HEADS * GLA_DK), 0.1),
        "gla_norm_g": 1.0 + nrm(ks[5], (DEPTH, GLA_DV), 0.02),
        "diff_q_norm_g": 1.0 + nrm(ks[6], (DEPTH, DIFF_DH), 0.02),
        "diff_k_norm_g": 1.0 + nrm(ks[7], (DEPTH, DIFF_DH), 0.02),
        "diff_lam_q1": nrm(ks[8], (DEPTH, DIFF_DH), 0.1),
        "diff_lam_k1": nrm(ks[9], (DEPTH, DIFF_DH), 0.1),
        "diff_lam_q2": nrm(ks[10], (DEPTH, DIFF_DH), 0.1),
        "diff_lam_k2": nrm(ks[11], (DEPTH, DIFF_DH), 0.1),
        "diff_sub_g": 1.0 + nrm(ks[12], (DEPTH, DIFF_DV), 0.02),
        "w_out": nrm(ks[13], (DEPTH, D_MODEL, D_MODEL), D_MODEL ** -0.5 * res_scale),
        "norm2_g": 1.0 + nrm(ks[14], (DEPTH, D_MODEL), 0.02),
        "ffn_w_up": nrm(ks[15], (DEPTH, D_MODEL, 2 * D_FF), D_MODEL ** -0.5),
        "ffn_conv_w": nrm(ks[16], (DEPTH, CONV_W, 2 * D_FF), CONV_W ** -0.5),
        "ffn_conv_b": nrm(ks[17], (DEPTH, 2 * D_FF), 0.02),
        "ffn_w_down": nrm(ks[18], (DEPTH, D_FF, D_MODEL), D_FF ** -0.5 * res_scale),
    }


def reference(x, norm1_g, w_in, gla_w_gk_up, gla_b_gk, gla_norm_g, diff_q_norm_g, diff_k_norm_g,
              diff_lam_q1, diff_lam_k1, diff_lam_q2, diff_lam_k2, diff_sub_g, w_out, norm2_g,
              ffn_w_up, ffn_conv_w, ffn_conv_b, ffn_w_down):
    for l in range(DEPTH):
        lambda_init = 0.8 - 0.6 * math.exp(-0.3 * l)
        h = rms_norm(x, norm1_g[l])
        g_q, g_k, g_v, g_r, g_low, d_q, d_k, d_v = split_cols(h @ w_in[l])
        o_gla = gla_mixer(g_q, g_k, g_v, g_r, g_low, gla_w_gk_up[l], gla_b_gk[l], gla_norm_g[l])
        o_diff = diff_mixer(d_q, d_k, d_v, diff_q_norm_g[l], diff_k_norm_g[l],
                            diff_lam_q1[l], diff_lam_k1[l], diff_lam_q2[l], diff_lam_k2[l],
                            diff_sub_g[l], lambda_init)
        x = x + jnp.concatenate([o_gla, o_diff], axis=-1) @ w_out[l]
        h2 = rms_norm(x, norm2_g[l])
        x = x + conv_ffn(h2, ffn_w_up[l], ffn_conv_w[l], ffn_conv_b[l], ffn_w_down[l])
    return x
```

```python
import functools
import math

import jax
import jax.numpy as jnp
from jax import lax
from jax.experimental import pallas as pl
from jax.experimental.pallas import tpu as pltpu

F32 = jnp.float32
BF16 = jnp.bfloat16

EPS = 1e-6
CHUNK = 64
GLA_HEADS = 4
GLA_DK = 64
GLA_DV = 128
GLA_RANK = 16
GLA_TAU = 16.0
DIFF_HEADS = 4
DIFF_DH = 64
DIFF_DV = 128
CONV_W = 3

LANES = 128
SUBLANES = 8
VMEM_LIMIT = 56 * 1024 * 1024

TM_PROJ = 512
TS_GLA = 512
TQ = 256
TM_FFN = 512
FC = 256

NEG = -1e30


def _dot(a, b):
    return jnp.dot(a, b, preferred_element_type=F32)


def _dot_nt(a, b):
    return lax.dot_general(a, b, (((1,), (1,)), ((), ())), preferred_element_type=F32)


def _dot_tn(a, b):
    return lax.dot_general(a, b, (((0,), (0,)), ((), ())), preferred_element_type=F32)


def _const_spec(shape):
    nd = len(shape)
    return pl.BlockSpec(shape, lambda *_: (0,) * nd, pipeline_mode=pl.Buffered(1))


def _proj_kernel(x_ref, g1_ref, w_ref, wup_ref, bgk_ref, grp_ref, qg_ref, kg_ref,
                 gq_ref, gk_ref, gv_ref, gr_ref, la_ref, dq_ref, dk_ref, dv_ref):
    x = x_ref[...]
    ms = jnp.mean(x * x, axis=-1, keepdims=True)
    h = (x * lax.rsqrt(ms + EPS) * g1_ref[...]).astype(BF16)

    gq_ref[...] = (_dot(h, w_ref[:, 0:256]) * (GLA_DK ** -0.5)).astype(BF16)
    gk_ref[...] = _dot(h, w_ref[:, 256:512]).astype(BF16)
    gv_ref[...] = _dot(h, w_ref[:, 512:1024]).astype(BF16)
    gr_ref[...] = _dot(h, w_ref[:, 1024:1536]).astype(BF16)

    low = _dot(h, w_ref[:, 3072:3200]).astype(BF16)
    z = _dot(low, wup_ref[...]) + bgk_ref[...]
    logsig = jnp.minimum(z, 0.0) - jnp.log1p(jnp.exp(-jnp.abs(z)))
    la_ref[...] = logsig * (1.0 / GLA_TAU)

    def qk_norm(p, g_ref, scale):
        msq = _dot((p * p).astype(BF16), grp_ref[...])
        return (p * lax.rsqrt(msq + EPS) * (g_ref[...] * scale)).astype(BF16)

    dq_ref[...] = qk_norm(_dot(h, w_ref[:, 1536:2048]), qg_ref, DIFF_DH ** -0.5)
    dk_ref[...] = qk_norm(_dot(h, w_ref[:, 2048:2560]), kg_ref, 1.0)
    dv_ref[...] = _dot(h, w_ref[:, 2560:3072]).astype(BF16)


def _proj_call(x2, g1, w, wup, bgk, grp, qg, kg):
    m, d = x2.shape
    tm = TM_PROJ
    row = lambda n: pl.BlockSpec((tm, n), lambda i: (i, 0))
    outs = [(256, BF16), (256, BF16), (512, BF16), (512, BF16), (256, F32),
            (512, BF16), (512, BF16), (512, BF16)]
    return pl.pallas_call(
        _proj_kernel,
        grid=(m // tm,),
        in_specs=[row(d), _const_spec(g1.shape), _const_spec(w.shape), _const_spec(wup.shape),
                  _const_spec(bgk.shape), _const_spec(grp.shape), _const_spec(qg.shape),
                  _const_spec(kg.shape)],
        out_specs=[row(n) for n, _ in outs],
        out_shape=[jax.ShapeDtypeStruct((m, n), dt) for n, dt in outs],
        compiler_params=pltpu.CompilerParams(dimension_semantics=("arbitrary",),
                                             vmem_limit_bytes=VMEM_LIMIT),
        name="proj",
    )(x2, g1, w, wup, bgk, grp, qg, kg)


def _gla_kernel(q_ref, k_ref, v_ref, r_ref, la_ref, ng_ref, o_ref, st_ref):
    @pl.when(pl.program_id(1) == 0)
    def _():
        st_ref[...] = jnp.zeros_like(st_ref)

    ri = lax.broadcasted_iota(jnp.int32, (CHUNK, CHUNK), 0)
    ci = lax.broadcasted_iota(jnp.int32, (CHUNK, CHUNK), 1)
    tril = jnp.where(ci <= ri, 1.0, 0.0).astype(BF16)
    lane = lax.broadcasted_iota(jnp.int32, (CHUNK, LANES), 1)
    head_mask = (lane < GLA_DK, lane >= GLA_DK)
    ng = ng_ref[...]

    for c in range(TS_GLA // CHUNK):
        rows = slice(c * CHUNK, (c + 1) * CHUNK)
        la = la_ref[0, rows, :]
        la_hi = la.astype(BF16)
        la_lo = (la - la_hi.astype(F32)).astype(BF16)
        cum = _dot(tril, la_hi) + _dot(tril, la_lo)
        total = cum[CHUNK - 1:CHUNK, :]
        kdec = k_ref[0, rows, :].astype(F32) * jnp.exp(total - cum)
        decay = jnp.exp(total)
        q = q_ref[0, rows, :]
        for pair in range(GLA_HEADS // 2):
            lanes = slice(pair * LANES, (pair + 1) * LANES)
            kp = kdec[:, lanes]
            qp = q[:, lanes]
            kvt = None
            for hh in range(2):
                head = 2 * pair + hh
                kh = jnp.where(head_mask[hh], kp, 0.0).astype(BF16)
                vh = v_ref[0, rows, head * GLA_DV:(head + 1) * GLA_DV]
                t = _dot_tn(vh, kh)
                kvt = t if kvt is None else kvt + t
            st = st_ref[pair] * decay[:, lanes] + kvt
            st_ref[pair] = st
            stb = st.astype(BF16)
            for hh in range(2):
                head = 2 * pair + hh
                cols = slice(head * GLA_DV, (head + 1) * GLA_DV)
                qh = jnp.where(head_mask[hh], qp, jnp.zeros_like(qp))
                o = _dot_nt(qh, stb)
                o = o * lax.rsqrt(jnp.mean(o * o, axis=-1, keepdims=True) + EPS) * ng
                r = r_ref[0, rows, cols].astype(F32)
                o_ref[0, rows, cols] = (o * (r * jax.nn.sigmoid(r))).astype(BF16)


def _gla_call(gq, gk, gv, gr, la, ng):
    b, s, _ = gq.shape
    ts = TS_GLA
    blk = lambda n: pl.BlockSpec((1, ts, n), lambda i, j: (i, j, 0))
    return pl.pallas_call(
        _gla_kernel,
        grid=(b, s // ts),
        in_specs=[blk(256), blk(256), blk(512), blk(512), blk(256), _const_spec(ng.shape)],
        out_specs=blk(512),
        out_shape=jax.ShapeDtypeStruct((b, s, 512), BF16),
        scratch_shapes=[pltpu.VMEM((GLA_HEADS // 2, GLA_DV, 2 * GLA_DK), F32)],
        compiler_params=pltpu.CompilerParams(dimension_semantics=("arbitrary", "arbitrary"),
                                             vmem_limit_bytes=VMEM_LIMIT),
        name="gla",
    )(gq, gk, gv, gr, la, ng)


def _attn_kernel(lam_ref, sg_ref, q_ref, k_ref, v_ref, o_ref, *, lambda_init):
    qi = pl.program_id(2)
    q = q_ref[0]
    lane = lax.broadcasted_iota(jnp.int32, q.shape, 1)
    zero = jnp.zeros_like(q)
    qq = jnp.concatenate([jnp.where(lane < DIFF_DH, q, zero),
                          jnp.where(lane >= DIFF_DH, q, zero)], axis=0)

    def step(start, carry, masked):
        m, l, acc = carry
        k = k_ref[0, pl.ds(start, TQ), :]
        v = v_ref[0, pl.ds(start, TQ), :]
        s = _dot_nt(qq, k)
        if masked:
            rq = lax.broadcasted_iota(jnp.int32, s.shape, 0) % TQ
            ck = lax.broadcasted_iota(jnp.int32, s.shape, 1)
            s = jnp.where(ck // CHUNK <= rq // CHUNK, s, NEG)
        m_new = jnp.maximum(m, jnp.max(s, axis=-1, keepdims=True))
        p = jnp.exp(s - m_new)
        alpha = jnp.exp(m - m_new)
        l = alpha * l + jnp.sum(p, axis=-1, keepdims=True)
        acc = alpha * acc + _dot(p.astype(BF16), v)
        return m_new, l, acc

    init = (jnp.full((2 * TQ, 1), NEG, F32), jnp.zeros((2 * TQ, 1), F32),
            jnp.zeros((2 * TQ, DIFF_DV), F32))
    carry = lax.fori_loop(0, qi, lambda j, c: step(pl.multiple_of(j * TQ, TQ), c, False), init)
    m, l, acc = step(pl.multiple_of(qi * TQ, TQ), carry, True)

    lamv = lam_ref[...]
    lam = (jnp.exp(jnp.sum(lamv[0:1] * lamv[1:2], axis=-1, keepdims=True))
           - jnp.exp(jnp.sum(lamv[2:3] * lamv[3:4], axis=-1, keepdims=True)) + lambda_init)
    o = acc / l
    o = o[:TQ] - lam * o[TQ:]
    o = o * lax.rsqrt(jnp.mean(o * o, axis=-1, keepdims=True) + EPS) * sg_ref[...]
    o_ref[0] = (o * (1.0 - lambda_init)).astype(BF16)


def _attn_call(lamv, sg, dq, dk, dv, lambda_init):
    b, s, _ = dq.shape
    qblk = pl.BlockSpec((1, TQ, LANES), lambda i, h, j: (i, j, h))
    kvblk = pl.BlockSpec((1, s, LANES), lambda i, h, j: (i, 0, h))
    return pl.pallas_call(
        functools.partial(_attn_kernel, lambda_init=lambda_init),
        grid=(b, DIFF_HEADS, s // TQ),
        in_specs=[_const_spec(lamv.shape), _const_spec(sg.shape), qblk, kvblk, kvblk],
        out_specs=qblk,
        out_shape=jax.ShapeDtypeStruct((b, s, DIFF_HEADS * DIFF_DV), BF16),
        compiler_params=pltpu.CompilerParams(
            dimension_semantics=("arbitrary", "arbitrary", "arbitrary"),
            vmem_limit_bytes=VMEM_LIMIT),
        name="attn",
    )(lamv, sg, dq, dk, dv)


def _ffn_kernel(x_ref, og_ref, od_ref, wo_ref, g2_ref, wup_ref, cw_ref, cb_ref, wdn_ref,
                out_ref, carry_ref, h2_ref, acc_ref, u_ref):
    tm = TM_FFN
    nf = wup_ref.shape[0]

    @pl.when(pl.program_id(1) == 0)
    def _():
        carry_ref[...] = jnp.zeros_like(carry_ref)

    half = wo_ref.shape[0] // 2
    x1 = x_ref[0] + _dot(og_ref[0], wo_ref[0:half, :]) + _dot(od_ref[0], wo_ref[half:, :])
    ms = jnp.mean(x1 * x1, axis=-1, keepdims=True)
    h2_ref[...] = (x1 * lax.rsqrt(ms + EPS) * g2_ref[...]).astype(BF16)
    acc_ref[...] = x1

    def body(f, _):
        u = _dot(h2_ref[...], wup_ref[f])
        u_ref[0:SUBLANES, :] = carry_ref[f]
        u_ref[SUBLANES:SUBLANES + tm, :] = u
        carry_ref[f] = u[tm - SUBLANES:tm, :]
        cw = cw_ref[f]
        y = (cw[0:1] * u_ref[SUBLANES - 2:SUBLANES - 2 + tm, :]
             + cw[1:2] * u_ref[SUBLANES - 1:SUBLANES - 1 + tm, :]
             + cw[2:3] * u + cb_ref[f])
        gate = y[:, :FC]
        a = (gate * jax.nn.sigmoid(gate) * y[:, FC:]).astype(BF16)
        acc_ref[...] += _dot(a, wdn_ref[f])
        return 0

    lax.fori_loop(0, nf, body, 0)
    out_ref[0] = acc_ref[...]


def _ffn_call(x, og, od, wo, g2, wup, cw, cb, wdn):
    b, s, d = x.shape
    tm = TM_FFN
    nf = wup.shape[0]
    blk = lambda n: pl.BlockSpec((1, tm, n), lambda i, j: (i, j, 0))
    return pl.pallas_call(
        _ffn_kernel,
        grid=(b, s // tm),
        in_specs=[blk(d), blk(512), blk(512), _const_spec(wo.shape), _const_spec(g2.shape),
                  _const_spec(wup.shape), _const_spec(cw.shape), _const_spec(cb.shape),
                  _const_spec(wdn.shape)],
        out_specs=blk(d),
        out_shape=jax.ShapeDtypeStruct((b, s, d), F32),
        scratch_shapes=[pltpu.VMEM((nf, SUBLANES, 2 * FC), F32),
                        pltpu.VMEM((tm, d), BF16),
                        pltpu.VMEM((tm, d), F32),
                        pltpu.VMEM((tm + SUBLANES, 2 * FC), F32)],
        compiler_params=pltpu.CompilerParams(dimension_semantics=("arbitrary", "arbitrary"),
                                             vmem_limit_bytes=VMEM_LIMIT),
        name="ffn",
    )(x, og, od, wo, g2, wup, cw, cb, wdn)


def _prep_layer(l, norm1_g, w_in, gla_w_gk_up, gla_b_gk, gla_norm_g, diff_q_norm_g, diff_k_norm_g,
                diff_lam_q1, diff_lam_k1, diff_lam_q2, diff_lam_k2, diff_sub_g, w_out, norm2_g,
                ffn_w_up, ffn_conv_w, ffn_conv_b, ffn_w_down):
    w = w_in[l]
    d = w.shape[0]
    low0 = 1536
    low = jnp.pad(w[:, low0:low0 + GLA_RANK], ((0, 0), (0, LANES - GLA_RANK)))
    w_r = jnp.concatenate([w[:, :low0], w[:, low0 + GLA_RANK:], low], axis=1).astype(BF16)
    wup_gate = jnp.pad(gla_w_gk_up[l], ((0, LANES - GLA_RANK), (0, 0))).astype(BF16)
    gi = jnp.arange(512) // DIFF_DH
    grp = jnp.where(gi[:, None] == gi[None, :], 1.0 / DIFF_DH, 0.0).astype(BF16)
    d_ff = ffn_w_down.shape[1]
    nf = d_ff // FC
    wu = ffn_w_up[l]
    wup = jnp.concatenate([wu[:, :d_ff].reshape(d, nf, FC), wu[:, d_ff:].reshape(d, nf, FC)],
                          axis=2).transpose(1, 0, 2).astype(BF16)
    cw = ffn_conv_w[l]
    cw_r = jnp.concatenate([cw[:, :d_ff].reshape(CONV_W, nf, FC), cw[:, d_ff:].reshape(CONV_W, nf, FC)],
                           axis=2).transpose(1, 0, 2)
    cb = ffn_conv_b[l]
    cb_r = jnp.concatenate([cb[:d_ff].reshape(nf, 1, FC), cb[d_ff:].reshape(nf, 1, FC)], axis=2)
    return dict(
        g1=norm1_g[l][None, :], w=w_r, wup_gate=wup_gate, bgk=gla_b_gk[l][None, :], grp=grp,
        qg=jnp.tile(diff_q_norm_g[l], 2 * DIFF_HEADS)[None, :],
        kg=jnp.tile(diff_k_norm_g[l], 2 * DIFF_HEADS)[None, :],
        ng=gla_norm_g[l][None, :],
        lamv=jnp.stack([diff_lam_q1[l], diff_lam_k1[l], diff_lam_q2[l], diff_lam_k2[l]]),
        sg=diff_sub_g[l][None, :],
        wo=w_out[l].astype(BF16), g2=norm2_g[l][None, :],
        wup=wup, cw=cw_r, cb=cb_r, wdn=ffn_w_down[l].reshape(nf, FC, d).astype(BF16))


def kernel(x, norm1_g, w_in, gla_w_gk_up, gla_b_gk, gla_norm_g, diff_q_norm_g, diff_k_norm_g,
           diff_lam_q1, diff_lam_k1, diff_lam_q2, diff_lam_k2, diff_sub_g, w_out, norm2_g,
           ffn_w_up, ffn_conv_w, ffn_conv_b, ffn_w_down):
    b, s, d = x.shape
    depth = w_in.shape[0]
    assert s % TS_GLA == 0 and s % TQ == 0 and s % TM_FFN == 0 and (b * s) % TM_PROJ == 0
    assert ffn_w_down.shape[1] % FC == 0
    params = (norm1_g, w_in, gla_w_gk_up, gla_b_gk, gla_norm_g, diff_q_norm_g, diff_k_norm_g,
              diff_lam_q1, diff_lam_k1, diff_lam_q2, diff_lam_k2, diff_sub_g, w_out, norm2_g,
              ffn_w_up, ffn_conv_w, ffn_conv_b, ffn_w_down)
    for l in range(depth):
        p = _prep_layer(l, *params)
        lambda_init = 0.8 - 0.6 * math.exp(-0.3 * l)
        gq, gk, gv, gr, la, dq, dk, dv = _proj_call(
            x.reshape(b * s, d), p["g1"], p["w"], p["wup_gate"], p["bgk"], p["grp"], p["qg"], p["kg"])
        sh = lambda t: t.reshape(b, s, t.shape[-1])
        og = _gla_call(sh(gq), sh(gk), sh(gv), sh(gr), sh(la), p["ng"])
        od = _attn_call(p["lamv"], p["sg"], sh(dq), sh(dk), sh(dv), lambda_init)
        x = _ffn_call(x, og, od, p["wo"], p["g2"], p["wup"], p["cw"], p["cb"], p["wdn"])
    return x
```

```python
import functools
import math

import jax
import jax.numpy as jnp
from jax import lax
from jax.experimental import pallas as pl
from jax.experimental.pallas import tpu as pltpu

F32 = jnp.float32
BF16 = jnp.bfloat16

EPS = 1e-6
CHUNK = 64
GLA_HEADS = 4
GLA_DK = 64
GLA_DV = 128
GLA_RANK = 16
GLA_TAU = 16.0
DIFF_HEADS = 4
DIFF_DH = 64
DIFF_DV = 128
CONV_W = 3

LANES = 128
SUBLANES = 8
VMEM_LIMIT = 56 * 1024 * 1024

TM_PROJ = 512
TS_GLA = 512
TQ = 256
TM_FFN = 512
FC = 256

NEG = -1e30
LOG2E = math.log2(math.e)
FAST_SCORE_BOUND = 40.0


def _dot(a, b):
    return jnp.dot(a, b, preferred_element_type=F32)


def _dot_nt(a, b):
    return lax.dot_general(a, b, (((1,), (1,)), ((), ())), preferred_element_type=F32)


def _dot_tn(a, b):
    return lax.dot_general(a, b, (((0,), (0,)), ((), ())), preferred_element_type=F32)


def _const_spec(shape):
    nd = len(shape)
    return pl.BlockSpec(shape, lambda *_: (0,) * nd, pipeline_mode=pl.Buffered(1))


def _proj_kernel(x_ref, g1_ref, w_ref, wup_ref, bgk_ref, grp_ref, qg_ref, kg_ref,
                 gq_ref, gk_ref, gv_ref, gr_ref, la_ref, dq_ref, dk_ref, dv_ref):
    x = x_ref[...]
    ms = jnp.mean(x * x, axis=-1, keepdims=True)
    h = (x * lax.rsqrt(ms + EPS) * g1_ref[...]).astype(BF16)

    gq_ref[...] = (_dot(h, w_ref[:, 0:256]) * (GLA_DK ** -0.5)).astype(BF16)
    gk_ref[...] = _dot(h, w_ref[:, 256:512]).astype(BF16)
    gv_ref[...] = _dot(h, w_ref[:, 512:1024]).astype(BF16)
    gr_ref[...] = _dot(h, w_ref[:, 1024:1536]).astype(BF16)

    low = _dot(h, w_ref[:, 3072:3200]).astype(BF16)
    z = _dot(low, wup_ref[...]) + bgk_ref[...]
    logsig = jnp.minimum(z, 0.0) - jnp.log1p(jnp.exp(-jnp.abs(z)))
    la_ref[...] = logsig * (1.0 / GLA_TAU)

    def qk_norm(p, g_ref, scale):
        msq = _dot((p * p).astype(BF16), grp_ref[...])
        return (p * lax.rsqrt(msq + EPS) * (g_ref[...] * scale)).astype(BF16)

    dq_ref[...] = qk_norm(_dot(h, w_ref[:, 1536:2048]), qg_ref, DIFF_DH ** -0.5 * LOG2E)
    dk_ref[...] = qk_norm(_dot(h, w_ref[:, 2048:2560]), kg_ref, 1.0)
    dv_ref[...] = _dot(h, w_ref[:, 2560:3072]).astype(BF16)


def _proj_call(x2, g1, w, wup, bgk, grp, qg, kg):
    m, d = x2.shape
    tm = TM_PROJ
    row = lambda n: pl.BlockSpec((tm, n), lambda i: (i, 0))
    outs = [(256, BF16), (256, BF16), (512, BF16), (512, BF16), (256, F32),
            (512, BF16), (512, BF16), (512, BF16)]
    return pl.pallas_call(
        _proj_kernel,
        grid=(m // tm,),
        in_specs=[row(d), _const_spec(g1.shape), _const_spec(w.shape), _const_spec(wup.shape),
                  _const_spec(bgk.shape), _const_spec(grp.shape), _const_spec(qg.shape),
                  _const_spec(kg.shape)],
        out_specs=[row(n) for n, _ in outs],
        out_shape=[jax.ShapeDtypeStruct((m, n), dt) for n, dt in outs],
        compiler_params=pltpu.CompilerParams(dimension_semantics=("arbitrary",),
                                             vmem_limit_bytes=VMEM_LIMIT),
        name="proj",
    )(x2, g1, w, wup, bgk, grp, qg, kg)


def _gla_kernel(q_ref, k_ref, v_ref, r_ref, la_ref, ng_ref, o_ref, st_ref):
    @pl.when(pl.program_id(1) == 0)
    def _():
        st_ref[...] = jnp.zeros_like(st_ref)

    ri = lax.broadcasted_iota(jnp.int32, (CHUNK, CHUNK), 0)
    ci = lax.broadcasted_iota(jnp.int32, (CHUNK, CHUNK), 1)
    tril = jnp.where(ci <= ri, 1.0, 0.0).astype(BF16)
    lane = lax.broadcasted_iota(jnp.int32, (CHUNK, LANES), 1)
    head_mask = (lane < GLA_DK, lane >= GLA_DK)
    ng = ng_ref[...]

    for c in range(TS_GLA // CHUNK):
        rows = slice(c * CHUNK, (c + 1) * CHUNK)
        la = la_ref[0, rows, :]
        la_hi = la.astype(BF16)
        la_lo = (la - la_hi.astype(F32)).astype(BF16)
        cum = _dot(tril, la_hi) + _dot(tril, la_lo)
        total = cum[CHUNK - 1:CHUNK, :]
        kdec = k_ref[0, rows, :].astype(F32) * jnp.exp(total - cum)
        decay = jnp.exp(total)
        q = q_ref[0, rows, :]
        for pair in range(GLA_HEADS // 2):
            lanes = slice(pair * LANES, (pair + 1) * LANES)
            kp = kdec[:, lanes]
            qp = q[:, lanes]
            kvt = None
            for hh in range(2):
                head = 2 * pair + hh
                kh = jnp.where(head_mask[hh], kp, 0.0).astype(BF16)
                vh = v_ref[0, rows, head * GLA_DV:(head + 1) * GLA_DV]
                t = _dot_tn(vh, kh)
                kvt = t if kvt is None else kvt + t
            st = st_ref[pair] * decay[:, lanes] + kvt
            st_ref[pair] = st
            stb = st.astype(BF16)
            for hh in range(2):
                head = 2 * pair + hh
                cols = slice(head * GLA_DV, (head + 1) * GLA_DV)
                qh = jnp.where(head_mask[hh], qp, jnp.zeros_like(qp))
                o = _dot_nt(qh, stb)
                o = o * lax.rsqrt(jnp.mean(o * o, axis=-1, keepdims=True) + EPS) * ng
                r = r_ref[0, rows, cols].astype(F32)
                o_ref[0, rows, cols] = (o * (r * jax.nn.sigmoid(r))).astype(BF16)


def _gla_call(gq, gk, gv, gr, la, ng):
    b, s, _ = gq.shape
    ts = TS_GLA
    blk = lambda n: pl.BlockSpec((1, ts, n), lambda i, j: (i, j, 0))
    return pl.pallas_call(
        _gla_kernel,
        grid=(b, s // ts),
        in_specs=[blk(256), blk(256), blk(512), blk(512), blk(256), _const_spec(ng.shape)],
        out_specs=blk(512),
        out_shape=jax.ShapeDtypeStruct((b, s, 512), BF16),
        scratch_shapes=[pltpu.VMEM((GLA_HEADS // 2, GLA_DV, 2 * GLA_DK), F32)],
        compiler_params=pltpu.CompilerParams(dimension_semantics=("arbitrary", "arbitrary"),
                                             vmem_limit_bytes=VMEM_LIMIT),
        name="gla",
    )(gq, gk, gv, gr, la, ng)


def _attn_kernel(flag_ref, lam_ref, sg_ref, q_ref, k_ref, v_ref, o_ref, qq_ref, acc_ref, l_ref, *,
                 lambda_init):
    nq = q_ref.shape[1] // TQ

    def rows(tile):
        return pl.ds(pl.multiple_of(tile * TQ, TQ), TQ)

    def stacked_q(tile):
        q = q_ref[0, rows(tile), :]
        lane = lax.broadcasted_iota(jnp.int32, q.shape, 1)
        zero = jnp.zeros_like(q)
        return jnp.concatenate([jnp.where(lane < DIFF_DH, q, zero),
                                jnp.where(lane >= DIFF_DH, q, zero)], axis=0)

    def scores(qq, kv_tile, masked):
        s = _dot_nt(qq, k_ref[0, rows(kv_tile), :])
        if masked:
            rq = lax.broadcasted_iota(jnp.int32, s.shape, 0) % TQ
            ck = lax.broadcasted_iota(jnp.int32, s.shape, 1)
            s = jnp.where(ck // CHUNK <= rq // CHUNK, s, NEG)
        return s

    def finish(tile, acc, l):
        lamv = lam_ref[...]
        lam = (jnp.exp(jnp.sum(lamv[0:1] * lamv[1:2], axis=-1, keepdims=True))
               - jnp.exp(jnp.sum(lamv[2:3] * lamv[3:4], axis=-1, keepdims=True)) + lambda_init)
        o = acc / l
        o = o[:TQ] - lam * o[TQ:]
        o = o * lax.rsqrt(jnp.mean(o * o, axis=-1, keepdims=True) + EPS) * sg_ref[...]
        o_ref[0, rows(tile), :] = (o * (1.0 - lambda_init)).astype(BF16)

    @pl.when(flag_ref[0] == 1)
    def _():
        def block(w, kv_tile, masked):
            p = jnp.exp2(scores(qq_ref[w], kv_tile, masked))
            l_ref[w] += p[:, :LANES] + p[:, LANES:]
            acc_ref[w] += _dot(p.astype(BF16), v_ref[0, rows(kv_tile), :])

        def pair(t, _):
            tiles = (t, nq - 1 - t)
            for w in range(2):
                qq_ref[w] = stacked_q(tiles[w])
            acc_ref[...] = jnp.zeros_like(acc_ref)
            l_ref[...] = jnp.zeros_like(l_ref)
            for j in range(nq - 1):
                w = (j >= t).astype(jnp.int32)
                block(w, j - w * t, False)
            for w in range(2):
                block(w, tiles[w], True)
            for w in range(2):
                finish(tiles[w], acc_ref[w], jnp.sum(l_ref[w], axis=-1, keepdims=True))
            return 0

        lax.fori_loop(0, nq // 2, pair, 0)

    @pl.when(flag_ref[0] == 0)
    def _():
        def tile_body(t, _):
            qq = stacked_q(t)

            def step(kv_tile, carry, masked):
                m, l, acc = carry
                s = scores(qq, kv_tile, masked)
                m_new = jnp.maximum(m, jnp.max(s, axis=-1, keepdims=True))
                p = jnp.exp2(s - m_new)
                alpha = jnp.exp2(m - m_new)
                l = alpha * l + jnp.sum(p, axis=-1, keepdims=True)
                acc = alpha * acc + _dot(p.astype(BF16), v_ref[0, rows(kv_tile), :])
                return m_new, l, acc

            init = (jnp.full((2 * TQ, 1), NEG, F32), jnp.zeros((2 * TQ, 1), F32),
                    jnp.zeros((2 * TQ, DIFF_DV), F32))
            carry = lax.fori_loop(0, t, lambda j, c: step(j, c, False), init)
            _, l, acc = step(t, carry, True)
            finish(t, acc, l)
            return 0

        lax.fori_loop(0, nq, tile_body, 0)


def _attn_call(flag, lamv, sg, dq, dk, dv, lambda_init):
    b, s, _ = dq.shape
    assert (s // TQ) % 2 == 0
    blk = pl.BlockSpec((1, s, LANES), lambda i, h, _: (i, 0, h))
    return pl.pallas_call(
        functools.partial(_attn_kernel, lambda_init=lambda_init),
        grid_spec=pltpu.PrefetchScalarGridSpec(
            num_scalar_prefetch=1,
            grid=(b, DIFF_HEADS),
            in_specs=[_const_spec(lamv.shape), _const_spec(sg.shape), blk, blk, blk],
            out_specs=blk,
            scratch_shapes=[pltpu.VMEM((2, 2 * TQ, LANES), BF16),
                            pltpu.VMEM((2, 2 * TQ, DIFF_DV), F32),
                            pltpu.VMEM((2, 2 * TQ, LANES), F32)]),
        out_shape=jax.ShapeDtypeStruct((b, s, DIFF_HEADS * DIFF_DV), BF16),
        compiler_params=pltpu.CompilerParams(
            dimension_semantics=("arbitrary", "arbitrary"),
            vmem_limit_bytes=VMEM_LIMIT),
        name="attn",
    )(flag, lamv, sg, dq, dk, dv)


def _ffn_kernel(x_ref, og_ref, od_ref, wo_ref, g2_ref, wup_ref, cw_ref, cb_ref, wdn_ref,
                out_ref, carry_ref, h2_ref, acc_ref, u_ref):
    tm = TM_FFN
    nf = wup_ref.shape[0]

    @pl.when(pl.program_id(1) == 0)
    def _():
        carry_ref[...] = jnp.zeros_like(carry_ref)

    half = wo_ref.shape[0] // 2
    x1 = x_ref[0] + _dot(og_ref[0], wo_ref[0:half, :]) + _dot(od_ref[0], wo_ref[half:, :])
    ms = jnp.mean(x1 * x1, axis=-1, keepdims=True)
    h2_ref[...] = (x1 * lax.rsqrt(ms + EPS) * g2_ref[...]).astype(BF16)
    acc_ref[...] = x1

    def body(f, _):
        u = _dot(h2_ref[...], wup_ref[f])
        u_ref[0:SUBLANES, :] = carry_ref[f]
        u_ref[SUBLANES:SUBLANES + tm, :] = u
        carry_ref[f] = u[tm - SUBLANES:tm, :]
        cw = cw_ref[f]
        y = (cw[0:1] * u_ref[SUBLANES - 2:SUBLANES - 2 + tm, :]
             + cw[1:2] * u_ref[SUBLANES - 1:SUBLANES - 1 + tm, :]
             + cw[2:3] * u + cb_ref[f])
        gate = y[:, :FC]
        a = (gate * jax.nn.sigmoid(gate) * y[:, FC:]).astype(BF16)
        acc_ref[...] += _dot(a, wdn_ref[f])
        return 0

    lax.fori_loop(0, nf, body, 0)
    out_ref[0] = acc_ref[...]


def _ffn_call(x, og, od, wo, g2, wup, cw, cb, wdn):
    b, s, d = x.shape
    tm = TM_FFN
    nf = wup.shape[0]
    blk = lambda n: pl.BlockSpec((1, tm, n), lambda i, j: (i, j, 0))
    return pl.pallas_call(
        _ffn_kernel,
        grid=(b, s // tm),
        in_specs=[blk(d), blk(512), blk(512), _const_spec(wo.shape), _const_spec(g2.shape),
                  _const_spec(wup.shape), _const_spec(cw.shape), _const_spec(cb.shape),
                  _const_spec(wdn.shape)],
        out_specs=blk(d),
        out_shape=jax.ShapeDtypeStruct((b, s, d), F32),
        scratch_shapes=[pltpu.VMEM((nf, SUBLANES, 2 * FC), F32),
                        pltpu.VMEM((tm, d), BF16),
                        pltpu.VMEM((tm, d), F32),
                        pltpu.VMEM((tm + SUBLANES, 2 * FC), F32)],
        compiler_params=pltpu.CompilerParams(dimension_semantics=("arbitrary", "arbitrary"),
                                             vmem_limit_bytes=VMEM_LIMIT),
        name="ffn",
    )(x, og, od, wo, g2, wup, cw, cb, wdn)


def _prep_layer(l, norm1_g, w_in, gla_w_gk_up, gla_b_gk, gla_norm_g, diff_q_norm_g, diff_k_norm_g,
                diff_lam_q1, diff_lam_k1, diff_lam_q2, diff_lam_k2, diff_sub_g, w_out, norm2_g,
                ffn_w_up, ffn_conv_w, ffn_conv_b, ffn_w_down):
    w = w_in[l]
    d = w.shape[0]
    low0 = 1536
    low = jnp.pad(w[:, low0:low0 + GLA_RANK], ((0, 0), (0, LANES - GLA_RANK)))
    w_r = jnp.concatenate([w[:, :low0], w[:, low0 + GLA_RANK:], low], axis=1).astype(BF16)
    wup_gate = jnp.pad(gla_w_gk_up[l], ((0, LANES - GLA_RANK), (0, 0))).astype(BF16)
    gi = jnp.arange(512) // DIFF_DH
    grp = jnp.where(gi[:, None] == gi[None, :], 1.0 / DIFF_DH, 0.0).astype(BF16)
    d_ff = ffn_w_down.shape[1]
    nf = d_ff // FC
    wu = ffn_w_up[l]
    wup = jnp.concatenate([wu[:, :d_ff].reshape(d, nf, FC), wu[:, d_ff:].reshape(d, nf, FC)],
                          axis=2).transpose(1, 0, 2).astype(BF16)
    cw = ffn_conv_w[l]
    cw_r = jnp.concatenate([cw[:, :d_ff].reshape(CONV_W, nf, FC), cw[:, d_ff:].reshape(CONV_W, nf, FC)],
                           axis=2).transpose(1, 0, 2)
    cb = ffn_conv_b[l]
    cb_r = jnp.concatenate([cb[:d_ff].reshape(nf, 1, FC), cb[d_ff:].reshape(nf, 1, FC)], axis=2)
    return dict(
        g1=norm1_g[l][None, :], w=w_r, wup_gate=wup_gate, bgk=gla_b_gk[l][None, :], grp=grp,
        qg=jnp.tile(diff_q_norm_g[l], 2 * DIFF_HEADS)[None, :],
        kg=jnp.tile(diff_k_norm_g[l], 2 * DIFF_HEADS)[None, :],
        ng=gla_norm_g[l][None, :],
        lamv=jnp.stack([diff_lam_q1[l], diff_lam_k1[l], diff_lam_q2[l], diff_lam_k2[l]]),
        fast=(1.02 * DIFF_DH ** 0.5 * jnp.max(jnp.abs(diff_q_norm_g[l])) * jnp.max(jnp.abs(diff_k_norm_g[l]))
              <= FAST_SCORE_BOUND).astype(jnp.int32).reshape(1),
        sg=diff_sub_g[l][None, :],
        wo=w_out[l].astype(BF16), g2=norm2_g[l][None, :],
        wup=wup, cw=cw_r, cb=cb_r, wdn=ffn_w_down[l].reshape(nf, FC, d).astype(BF16))


def kernel(x, norm1_g, w_in, gla_w_gk_up, gla_b_gk, gla_norm_g, diff_q_norm_g, diff_k_norm_g,
           diff_lam_q1, diff_lam_k1, diff_lam_q2, diff_lam_k2, diff_sub_g, w_out, norm2_g,
           ffn_w_up, ffn_conv_w, ffn_conv_b, ffn_w_down):
    b, s, d = x.shape
    depth = w_in.shape[0]
    assert s % TS_GLA == 0 and s % TQ == 0 and s % TM_FFN == 0 and (b * s) % TM_PROJ == 0
    assert ffn_w_down.shape[1] % FC == 0
    params = (norm1_g, w_in, gla_w_gk_up, gla_b_gk, gla_norm_g, diff_q_norm_g, diff_k_norm_g,
              diff_lam_q1, diff_lam_k1, diff_lam_q2, diff_lam_k2, diff_sub_g, w_out, norm2_g,
              ffn_w_up, ffn_conv_w, ffn_conv_b, ffn_w_down)
    for l in range(depth):
        p = _prep_layer(l, *params)
        lambda_init = 0.8 - 0.6 * math.exp(-0.3 * l)
        gq, gk, gv, gr, la, dq, dk, dv = _proj_call(
            x.reshape(b * s, d), p["g1"], p["w"], p["wup_gate"], p["bgk"], p["grp"], p["qg"], p["kg"])
        sh = lambda t: t.reshape(b, s, t.shape[-1])
        og = _gla_call(sh(gq), sh(gk), sh(gv), sh(gr), sh(la), p["ng"])
        od = _attn_call(p["fast"], p["lamv"], p["sg"], sh(dq), sh(dk), sh(dv), lambda_init)
        x = _ffn_call(x, og, od, p["wo"], p["g2"], p["wup"], p["cw"], p["cb"], p["wdn"])
    return x
```

```python
import functools
import math

import jax
import jax.numpy as jnp
from jax import lax
from jax.experimental import pallas as pl
from jax.experimental.pallas import tpu as pltpu

F32 = jnp.float32
BF16 = jnp.bfloat16

EPS = 1e-6
CHUNK = 64
GLA_HEADS = 4
GLA_DK = 64
GLA_DV = 128
GLA_RANK = 16
GLA_TAU = 16.0
DIFF_HEADS = 4
DIFF_DH = 64
DIFF_DV = 128
CONV_W = 3

LANES = 128
SUBLANES = 8
VMEM_LIMIT = 56 * 1024 * 1024

TM_PROJ = 512
TS_GLA = 512
TQ = 256
TM_FFN = 512
FC = 256

NEG = -1e30
LOG2E = math.log2(math.e)
FAST_SCORE_BOUND = 40.0


def _dot(a, b):
    return jnp.dot(a, b, preferred_element_type=F32)


def _dot_nt(a, b):
    return lax.dot_general(a, b, (((1,), (1,)), ((), ())), preferred_element_type=F32)


def _dot_tn(a, b):
    return lax.dot_general(a, b, (((0,), (0,)), ((), ())), preferred_element_type=F32)


def _const_spec(shape):
    nd = len(shape)
    return pl.BlockSpec(shape, lambda *_: (0,) * nd, pipeline_mode=pl.Buffered(1))


def _proj_kernel(x_ref, g1_ref, w_ref, wup_ref, bgk_ref, grp_ref, qg_ref, kg_ref,
                 gq_ref, gk_ref, gv_ref, gr_ref, la_ref, dq_ref, dk_ref, dv_ref):
    x = x_ref[...]
    ms = jnp.mean(x * x, axis=-1, keepdims=True)
    h = (x * lax.rsqrt(ms + EPS) * g1_ref[...]).astype(BF16)

    gq_ref[...] = (_dot(h, w_ref[:, 0:256]) * (GLA_DK ** -0.5)).astype(BF16)
    gk_ref[...] = _dot(h, w_ref[:, 256:512]).astype(BF16)
    gv_ref[...] = _dot(h, w_ref[:, 512:1024]).astype(BF16)
    gr_ref[...] = _dot(h, w_ref[:, 1024:1536]).astype(BF16)

    low = _dot(h, w_ref[:, 3072:3200]).astype(BF16)
    z = _dot(low, wup_ref[...]) + bgk_ref[...]
    logsig = jnp.minimum(z, 0.0) - jnp.log1p(jnp.exp(-jnp.abs(z)))
    la_ref[...] = logsig * (1.0 / GLA_TAU)

    def qk_norm(p, g_ref, scale):
        msq = _dot((p * p).astype(BF16), grp_ref[...])
        return (p * lax.rsqrt(msq + EPS) * (g_ref[...] * scale)).astype(BF16)

    dq_ref[...] = qk_norm(_dot(h, w_ref[:, 1536:2048]), qg_ref, DIFF_DH ** -0.5 * LOG2E)
    dk_ref[...] = qk_norm(_dot(h, w_ref[:, 2048:2560]), kg_ref, 1.0)
    dv_ref[...] = _dot(h, w_ref[:, 2560:3072]).astype(BF16)


def _proj_call(x2, g1, w, wup, bgk, grp, qg, kg):
    m, d = x2.shape
    tm = TM_PROJ
    row = lambda n: pl.BlockSpec((tm, n), lambda i: (i, 0))
    outs = [(256, BF16), (256, BF16), (512, BF16), (512, BF16), (256, F32),
            (512, BF16), (512, BF16), (512, BF16)]
    return pl.pallas_call(
        _proj_kernel,
        grid=(m // tm,),
        in_specs=[row(d), _const_spec(g1.shape), _const_spec(w.shape), _const_spec(wup.shape),
                  _const_spec(bgk.shape), _const_spec(grp.shape), _const_spec(qg.shape),
                  _const_spec(kg.shape)],
        out_specs=[row(n) for n, _ in outs],
        out_shape=[jax.ShapeDtypeStruct((m, n), dt) for n, dt in outs],
        compiler_params=pltpu.CompilerParams(dimension_semantics=("arbitrary",),
                                             vmem_limit_bytes=VMEM_LIMIT),
        name="proj",
    )(x2, g1, w, wup, bgk, grp, qg, kg)


def _gla_kernel(q_ref, k_ref, v_ref, r_ref, la_ref, ng_ref, o_ref, st_ref, kvt_ref, stb_ref):
    @pl.when(pl.program_id(1) == 0)
    def _():
        st_ref[...] = jnp.zeros_like(st_ref)

    ri = lax.broadcasted_iota(jnp.int32, (CHUNK, CHUNK), 0)
    ci = lax.broadcasted_iota(jnp.int32, (CHUNK, CHUNK), 1)
    tril = jnp.where(ci <= ri, 1.0, 0.0).astype(BF16)
    lane = lax.broadcasted_iota(jnp.int32, (CHUNK, LANES), 1)
    head_mask = (lane < GLA_DK, lane >= GLA_DK)
    ng = ng_ref[...]
    n_chunks = TS_GLA // CHUNK
    n_pairs = GLA_HEADS // 2
    chunk_rows = lambda c: slice(c * CHUNK, (c + 1) * CHUNK)
    pair_lanes = lambda p: slice(p * LANES, (p + 1) * LANES)

    decays = []
    for c in range(n_chunks):
        rows = chunk_rows(c)
        la = la_ref[0, rows, :]
        la_hi = la.astype(BF16)
        la_lo = (la - la_hi.astype(F32)).astype(BF16)
        cum = _dot(tril, la_hi) + _dot(tril, la_lo)
        total = cum[CHUNK - 1:CHUNK, :]
        kdec = k_ref[0, rows, :].astype(F32) * jnp.exp(total - cum)
        decays.append(jnp.exp(total))
        for pair in range(n_pairs):
            kp = kdec[:, pair_lanes(pair)]
            kvt = None
            for hh in range(2):
                head = 2 * pair + hh
                kh = jnp.where(head_mask[hh], kp, 0.0).astype(BF16)
                vh = v_ref[0, rows, head * GLA_DV:(head + 1) * GLA_DV]
                t = _dot_tn(vh, kh)
                kvt = t if kvt is None else kvt + t
            kvt_ref[c, pair] = kvt

    for pair in range(n_pairs):
        st = st_ref[pair]
        for c in range(n_chunks):
            st = st * decays[c][:, pair_lanes(pair)] + kvt_ref[c, pair]
            stb_ref[c, pair] = st.astype(BF16)
        st_ref[pair] = st

    for c in range(n_chunks):
        rows = chunk_rows(c)
        q = q_ref[0, rows, :]
        for pair in range(n_pairs):
            qp = q[:, pair_lanes(pair)]
            stb = stb_ref[c, pair]
            for hh in range(2):
                head = 2 * pair + hh
                cols = slice(head * GLA_DV, (head + 1) * GLA_DV)
                qh = jnp.where(head_mask[hh], qp, jnp.zeros_like(qp))
                o = _dot_nt(qh, stb)
                o = o * lax.rsqrt(jnp.mean(o * o, axis=-1, keepdims=True) + EPS) * ng
                r = r_ref[0, rows, cols].astype(F32)
                o_ref[0, rows, cols] = (o * (r * jax.nn.sigmoid(r))).astype(BF16)


def _gla_call(gq, gk, gv, gr, la, ng):
    b, s, _ = gq.shape
    ts = TS_GLA
    blk = lambda n: pl.BlockSpec((1, ts, n), lambda i, j: (i, j, 0))
    return pl.pallas_call(
        _gla_kernel,
        grid=(b, s // ts),
        in_specs=[blk(256), blk(256), blk(512), blk(512), blk(256), _const_spec(ng.shape)],
        out_specs=blk(512),
        out_shape=jax.ShapeDtypeStruct((b, s, 512), BF16),
        scratch_shapes=[pltpu.VMEM((GLA_HEADS // 2, GLA_DV, 2 * GLA_DK), F32),
                        pltpu.VMEM((ts // CHUNK, GLA_HEADS // 2, GLA_DV, 2 * GLA_DK), F32),
                        pltpu.VMEM((ts // CHUNK, GLA_HEADS // 2, GLA_DV, 2 * GLA_DK), BF16)],
        compiler_params=pltpu.CompilerParams(dimension_semantics=("arbitrary", "arbitrary"),
                                             vmem_limit_bytes=VMEM_LIMIT),
        name="gla",
    )(gq, gk, gv, gr, la, ng)


def _attn_kernel(flag_ref, lam_ref, sg_ref, q_ref, k_ref, v_ref, o_ref, qq_ref, acc_ref, l_ref, *,
                 lambda_init):
    nq = q_ref.shape[1] // TQ

    def rows(tile):
        return pl.ds(pl.multiple_of(tile * TQ, TQ), TQ)

    def stacked_q(tile):
        q = q_ref[0, rows(tile), :]
        lane = lax.broadcasted_iota(jnp.int32, q.shape, 1)
        zero = jnp.zeros_like(q)
        return jnp.concatenate([jnp.where(lane < DIFF_DH, q, zero),
                                jnp.where(lane >= DIFF_DH, q, zero)], axis=0)

    def scores(qq, kv_tile, masked):
        s = _dot_nt(qq, k_ref[0, rows(kv_tile), :])
        if masked:
            rq = lax.broadcasted_iota(jnp.int32, s.shape, 0) % TQ
            ck = lax.broadcasted_iota(jnp.int32, s.shape, 1)
            s = jnp.where(ck // CHUNK <= rq // CHUNK, s, NEG)
        return s

    def finish(tile, acc, l):
        lamv = lam_ref[...]
        lam = (jnp.exp(jnp.sum(lamv[0:1] * lamv[1:2], axis=-1, keepdims=True))
               - jnp.exp(jnp.sum(lamv[2:3] * lamv[3:4], axis=-1, keepdims=True)) + lambda_init)
        o = acc / l
        o = o[:TQ] - lam * o[TQ:]
        o = o * lax.rsqrt(jnp.mean(o * o, axis=-1, keepdims=True) + EPS) * sg_ref[...]
        o_ref[0, rows(tile), :] = (o * (1.0 - lambda_init)).astype(BF16)

    @pl.when(flag_ref[0] == 1)
    def _():
        def block(w, kv_tile, masked):
            p = jnp.exp2(scores(qq_ref[w], kv_tile, masked))
            l_ref[w] += p[:, :LANES] + p[:, LANES:]
            acc_ref[w] += _dot(p.astype(BF16), v_ref[0, rows(kv_tile), :])

        def pair(t, _):
            tiles = (t, nq - 1 - t)
            for w in range(2):
                qq_ref[w] = stacked_q(tiles[w])
            acc_ref[...] = jnp.zeros_like(acc_ref)
            l_ref[...] = jnp.zeros_like(l_ref)
            for j in range(nq - 1):
                w = jnp.where(j >= t, 1, 0)
                block(w, j - w * t, False)
            for w in range(2):
                block(w, tiles[w], True)
            for w in range(2):
                finish(tiles[w], acc_ref[w], jnp.sum(l_ref[w], axis=-1, keepdims=True))
            return 0

        lax.fori_loop(0, nq // 2, pair, 0)

    @pl.when(flag_ref[0] == 0)
    def _():
        def tile_body(t, _):
            qq = stacked_q(t)

            def step(kv_tile, carry, masked):
                m, l, acc = carry
                s = scores(qq, kv_tile, masked)
                m_new = jnp.maximum(m, jnp.max(s, axis=-1, keepdims=True))
                p = jnp.exp2(s - m_new)
                alpha = jnp.exp2(m - m_new)
                l = alpha * l + jnp.sum(p, axis=-1, keepdims=True)
                acc = alpha * acc + _dot(p.astype(BF16), v_ref[0, rows(kv_tile), :])
                return m_new, l, acc

            init = (jnp.full((2 * TQ, 1), NEG, F32), jnp.zeros((2 * TQ, 1), F32),
                    jnp.zeros((2 * TQ, DIFF_DV), F32))
            carry = lax.fori_loop(0, t, lambda j, c: step(j, c, False), init)
            _, l, acc = step(t, carry, True)
            finish(t, acc, l)
            return 0

        lax.fori_loop(0, nq, tile_body, 0)


def _attn_call(flag, lamv, sg, dq, dk, dv, lambda_init):
    b, s, _ = dq.shape
    assert (s // TQ) % 2 == 0
    blk = pl.BlockSpec((1, s, LANES), lambda i, h, _: (i, 0, h))
    return pl.pallas_call(
        functools.partial(_attn_kernel, lambda_init=lambda_init),
        grid_spec=pltpu.PrefetchScalarGridSpec(
            num_scalar_prefetch=1,
            grid=(b, DIFF_HEADS),
            in_specs=[_const_spec(lamv.shape), _const_spec(sg.shape), blk, blk, blk],
            out_specs=blk,
            scratch_shapes=[pltpu.VMEM((2, 2 * TQ, LANES), BF16),
                            pltpu.VMEM((2, 2 * TQ, DIFF_DV), F32),
                            pltpu.VMEM((2, 2 * TQ, LANES), F32)]),
        out_shape=jax.ShapeDtypeStruct((b, s, DIFF_HEADS * DIFF_DV), BF16),
        compiler_params=pltpu.CompilerParams(
            dimension_semantics=("arbitrary", "arbitrary"),
            vmem_limit_bytes=VMEM_LIMIT),
        name="attn",
    )(flag, lamv, sg, dq, dk, dv)


def _ffn_kernel(x_ref, og_ref, od_ref, wo_ref, g2_ref, wup_ref, cw_ref, cb_ref, wdn_ref,
                out_ref, carry_ref, h2_ref, acc_ref, u0_ref, u1_ref):
    tm = TM_FFN
    nf = wup_ref.shape[0]
    assert nf % 2 == 1

    @pl.when(pl.program_id(1) == 0)
    def _():
        carry_ref[...] = jnp.zeros_like(carry_ref)

    half = wo_ref.shape[0] // 2
    x1 = x_ref[0] + _dot(og_ref[0], wo_ref[0:half, :]) + _dot(od_ref[0], wo_ref[half:, :])
    ms = jnp.mean(x1 * x1, axis=-1, keepdims=True)
    h2_ref[...] = (x1 * lax.rsqrt(ms + EPS) * g2_ref[...]).astype(BF16)
    acc_ref[...] = x1

    def up(f, u_ref):
        u = _dot(h2_ref[...], wup_ref[f])
        u_ref[0:SUBLANES, :] = carry_ref[f]
        u_ref[SUBLANES:SUBLANES + tm, :] = u
        carry_ref[f] = u[tm - SUBLANES:tm, :]

    def act(f, u_ref):
        cw = cw_ref[f]
        y = (cw[0:1] * u_ref[SUBLANES - 2:SUBLANES - 2 + tm, :]
             + cw[1:2] * u_ref[SUBLANES - 1:SUBLANES - 1 + tm, :]
             + cw[2:3] * u_ref[SUBLANES:SUBLANES + tm, :] + cb_ref[f])
        gate = y[:, :FC]
        return (gate * jax.nn.sigmoid(gate) * y[:, FC:]).astype(BF16)

    def down(f, u_ref):
        acc_ref[...] += _dot(act(f, u_ref), wdn_ref[f])

    up(0, u0_ref)

    def body(i, _):
        f = 2 * i
        up(f + 1, u1_ref)
        down(f, u0_ref)
        up(f + 2, u0_ref)
        down(f + 1, u1_ref)
        return 0

    lax.fori_loop(0, nf // 2, body, 0)
    down(nf - 1, u0_ref)
    out_ref[0] = acc_ref[...]


def _ffn_call(x, og, od, wo, g2, wup, cw, cb, wdn):
    b, s, d = x.shape
    tm = TM_FFN
    nf = wup.shape[0]
    blk = lambda n: pl.BlockSpec((1, tm, n), lambda i, j: (i, j, 0))
    return pl.pallas_call(
        _ffn_kernel,
        grid=(b, s // tm),
        in_specs=[blk(d), blk(512), blk(512), _const_spec(wo.shape), _const_spec(g2.shape),
                  _const_spec(wup.shape), _const_spec(cw.shape), _const_spec(cb.shape),
                  _const_spec(wdn.shape)],
        out_specs=blk(d),
        out_shape=jax.ShapeDtypeStruct((b, s, d), F32),
        scratch_shapes=[pltpu.VMEM((nf, SUBLANES, 2 * FC), F32),
                        pltpu.VMEM((tm, d), BF16),
                        pltpu.VMEM((tm, d), F32),
                        pltpu.VMEM((tm + SUBLANES, 2 * FC), F32),
                        pltpu.VMEM((tm + SUBLANES, 2 * FC), F32)],
        compiler_params=pltpu.CompilerParams(dimension_semantics=("arbitrary", "arbitrary"),
                                             vmem_limit_bytes=VMEM_LIMIT),
        name="ffn",
    )(x, og, od, wo, g2, wup, cw, cb, wdn)


def _prep_layer(l, norm1_g, w_in, gla_w_gk_up, gla_b_gk, gla_norm_g, diff_q_norm_g, diff_k_norm_g,
                diff_lam_q1, diff_lam_k1, diff_lam_q2, diff_lam_k2, diff_sub_g, w_out, norm2_g,
                ffn_w_up, ffn_conv_w, ffn_conv_b, ffn_w_down):
    w = w_in[l]
    d = w.shape[0]
    low0 = 1536
    low = jnp.pad(w[:, low0:low0 + GLA_RANK], ((0, 0), (0, LANES - GLA_RANK)))
    w_r = jnp.concatenate([w[:, :low0], w[:, low0 + GLA_RANK:], low], axis=1).astype(BF16)
    wup_gate = jnp.pad(gla_w_gk_up[l], ((0, LANES - GLA_RANK), (0, 0))).astype(BF16)
    gi = jnp.arange(512) // DIFF_DH
    grp = jnp.where(gi[:, None] == gi[None, :], 1.0 / DIFF_DH, 0.0).astype(BF16)
    d_ff = ffn_w_down.shape[1]
    nf = d_ff // FC
    wu = ffn_w_up[l]
    wup = jnp.concatenate([wu[:, :d_ff].reshape(d, nf, FC), wu[:, d_ff:].reshape(d, nf, FC)],
                          axis=2).transpose(1, 0, 2).astype(BF16)
    cw = ffn_conv_w[l]
    cw_r = jnp.concatenate([cw[:, :d_ff].reshape(CONV_W, nf, FC), cw[:, d_ff:].reshape(CONV_W, nf, FC)],
                           axis=2).transpose(1, 0, 2)
    cb = ffn_conv_b[l]
    cb_r = jnp.concatenate([cb[:d_ff].reshape(nf, 1, FC), cb[d_ff:].reshape(nf, 1, FC)], axis=2)
    return dict(
        g1=norm1_g[l][None, :], w=w_r, wup_gate=wup_gate, bgk=gla_b_gk[l][None, :], grp=grp,
        qg=jnp.tile(diff_q_norm_g[l], 2 * DIFF_HEADS)[None, :],
        kg=jnp.tile(diff_k_norm_g[l], 2 * DIFF_HEADS)[None, :],
        ng=gla_norm_g[l][None, :],
        lamv=jnp.stack([diff_lam_q1[l], diff_lam_k1[l], diff_lam_q2[l], diff_lam_k2[l]]),
        fast=(1.02 * DIFF_DH ** 0.5 * jnp.max(jnp.abs(diff_q_norm_g[l])) * jnp.max(jnp.abs(diff_k_norm_g[l]))
              <= FAST_SCORE_BOUND).astype(jnp.int32).reshape(1),
        sg=diff_sub_g[l][None, :],
        wo=w_out[l].astype(BF16), g2=norm2_g[l][None, :],
        wup=wup, cw=cw_r, cb=cb_r, wdn=ffn_w_down[l].reshape(nf, FC, d).astype(BF16))


def kernel(x, norm1_g, w_in, gla_w_gk_up, gla_b_gk, gla_norm_g, diff_q_norm_g, diff_k_norm_g,
           diff_lam_q1, diff_lam_k1, diff_lam_q2, diff_lam_k2, diff_sub_g, w_out, norm2_g,
           ffn_w_up, ffn_conv_w, ffn_conv_b, ffn_w_down):
    b, s, d = x.shape
    depth = w_in.shape[0]
    assert s % TS_GLA == 0 and s % TQ == 0 and s % TM_FFN == 0 and (b * s) % TM_PROJ == 0
    assert ffn_w_down.shape[1] % FC == 0
    params = (norm1_g, w_in, gla_w_gk_up, gla_b_gk, gla_norm_g, diff_q_norm_g, diff_k_norm_g,
              diff_lam_q1, diff_lam_k1, diff_lam_q2, diff_lam_k2, diff_sub_g, w_out, norm2_g,
              ffn_w_up, ffn_conv_w, ffn_conv_b, ffn_w_down)
    for l in range(depth):
        p = _prep_layer(l, *params)
        lambda_init = 0.8 - 0.6 * math.exp(-0.3 * l)
        gq, gk, gv, gr, la, dq, dk, dv = _proj_call(
            x.reshape(b * s, d), p["g1"], p["w"], p["wup_gate"], p["bgk"], p["grp"], p["qg"], p["kg"])
        sh = lambda t: t.reshape(b, s, t.shape[-1])
        og = _gla_call(sh(gq), sh(gk), sh(gv), sh(gr), sh(la), p["ng"])
        od = _attn_call(p["fast"], p["lamv"], p["sg"], sh(dq), sh(dk), sh(dv), lambda_init)
        x = _ffn_call(x, og, od, p["wo"], p["g2"], p["wup"], p["cw"], p["cb"], p["wdn"])
    return x
```

```python
import functools
import math

import jax
import jax.numpy as jnp
from jax import lax
from jax.experimental import pallas as pl
from jax.experimental.pallas import tpu as pltpu

F32 = jnp.float32
BF16 = jnp.bfloat16

EPS = 1e-6
CHUNK = 64
GLA_HEADS = 4
GLA_DK = 64
GLA_DV = 128
GLA_RANK = 16
GLA_TAU = 16.0
DIFF_HEADS = 4
DIFF_DH = 64
DIFF_DV = 128
CONV_W = 3

LANES = 128
SUBLANES = 8
VMEM_LIMIT = 56 * 1024 * 1024

TM_MIX = 512
TQ = 256
TM_FFN = 512
FC = 256
ATTN_PAIRS_PER_STEP = 2

NEG = -1e30
LOG2E = math.log2(math.e)
FAST_SCORE_BOUND = 40.0


def _dot(a, b):
    return jnp.dot(a, b, preferred_element_type=F32)


def _dot_nt(a, b):
    return lax.dot_general(a, b, (((1,), (1,)), ((), ())), preferred_element_type=F32)


def _dot_tn(a, b):
    return lax.dot_general(a, b, (((0,), (0,)), ((), ())), preferred_element_type=F32)


def _const_spec(shape):
    nd = len(shape)
    return pl.BlockSpec(shape, lambda *_: (0,) * nd, pipeline_mode=pl.Buffered(1))


def _mix_in_kernel(x_ref, g1_ref, w_ref, wup_ref, bgk_ref, qg_ref, kg_ref, ng_ref,
                   og_ref, dq_ref, dk_ref, dv_ref,
                   q_ref, k_ref, v_ref, r_ref, la_ref, st_ref, kvt_ref, stb_ref):
    @pl.when(pl.program_id(1) == 0)
    def _():
        st_ref[...] = jnp.zeros_like(st_ref)

    x = x_ref[0]
    ms = jnp.mean(x * x, axis=-1, keepdims=True)
    h = (x * lax.rsqrt(ms + EPS) * g1_ref[...]).astype(BF16)

    q_ref[...] = (_dot(h, w_ref[:, 0:256]) * (GLA_DK ** -0.5)).astype(BF16)
    k_ref[...] = _dot(h, w_ref[:, 256:512]).astype(BF16)
    v_ref[...] = _dot(h, w_ref[:, 512:1024]).astype(BF16)
    r_ref[...] = _dot(h, w_ref[:, 1024:1536]).astype(BF16)

    low = _dot(h, w_ref[:, 3072:3200]).astype(BF16)
    z = _dot(low, wup_ref[...]) + bgk_ref[...]
    logsig = jnp.minimum(z, 0.0) - jnp.log1p(jnp.exp(-jnp.abs(z)))
    la_ref[...] = logsig * (1.0 / GLA_TAU)

    ri = lax.broadcasted_iota(jnp.int32, (CHUNK, CHUNK), 0)
    ci = lax.broadcasted_iota(jnp.int32, (CHUNK, CHUNK), 1)
    tril = jnp.where(ci <= ri, 1.0, 0.0).astype(BF16)
    lane = lax.broadcasted_iota(jnp.int32, (CHUNK, LANES), 1)
    head_mask = (lane < GLA_DK, lane >= GLA_DK)
    ng = ng_ref[...]
    n_chunks = TM_MIX // CHUNK
    n_pairs = GLA_HEADS // 2
    chunk_rows = lambda c: slice(c * CHUNK, (c + 1) * CHUNK)
    pair_lanes = lambda p: slice(p * LANES, (p + 1) * LANES)

    decays = []
    for c in range(n_chunks):
        rows = chunk_rows(c)
        la = la_ref[rows, :]
        la_hi = la.astype(BF16)
        la_lo = (la - la_hi.astype(F32)).astype(BF16)
        cum = _dot(tril, la_hi) + _dot(tril, la_lo)
        total = cum[CHUNK - 1:CHUNK, :]
        kdec = k_ref[rows, :].astype(F32) * jnp.exp(total - cum)
        decays.append(jnp.exp(total))
        for pair in range(n_pairs):
            kp = kdec[:, pair_lanes(pair)]
            kvt = None
            for hh in range(2):
                head = 2 * pair + hh
                kh = jnp.where(head_mask[hh], kp, 0.0).astype(BF16)
                vh = v_ref[rows, head * GLA_DV:(head + 1) * GLA_DV]
                t = _dot_tn(vh, kh)
                kvt = t if kvt is None else kvt + t
            kvt_ref[c, pair] = kvt

    for pair in range(n_pairs):
        st = st_ref[pair]
        for c in range(n_chunks):
            st = st * decays[c][:, pair_lanes(pair)] + kvt_ref[c, pair]
            stb_ref[c, pair] = st.astype(BF16)
        st_ref[pair] = st

    for c in range(n_chunks):
        rows = chunk_rows(c)
        q = q_ref[rows, :]
        for pair in range(n_pairs):
            qp = q[:, pair_lanes(pair)]
            stb = stb_ref[c, pair]
            for hh in range(2):
                head = 2 * pair + hh
                cols = slice(head * GLA_DV, (head + 1) * GLA_DV)
                qh = jnp.where(head_mask[hh], qp, jnp.zeros_like(qp))
                o = _dot_nt(qh, stb)
                o = o * lax.rsqrt(jnp.mean(o * o, axis=-1, keepdims=True) + EPS) * ng
                r = r_ref[rows, cols].astype(F32)
                og_ref[0, rows, cols] = (o * (r * jax.nn.sigmoid(r))).astype(BF16)

    def qk_norm(p, g_ref, scale):
        lo = lax.broadcasted_iota(jnp.int32, (p.shape[0], LANES), 1) < DIFF_DH
        blocks = []
        for j in range(p.shape[1] // LANES):
            blk = p[:, j * LANES:(j + 1) * LANES]
            sq = blk * blk
            s_lo = jnp.sum(jnp.where(lo, sq, 0.0), axis=-1, keepdims=True)
            s_hi = jnp.sum(jnp.where(lo, 0.0, sq), axis=-1, keepdims=True)
            blocks.append(blk * lax.rsqrt(jnp.where(lo, s_lo, s_hi) * (1.0 / DIFF_DH) + EPS))
        return (jnp.concatenate(blocks, axis=-1) * (g_ref[...] * scale)).astype(BF16)

    dq_ref[0] = qk_norm(_dot(h, w_ref[:, 1536:2048]), qg_ref, DIFF_DH ** -0.5 * LOG2E)
    dk_ref[0] = qk_norm(_dot(h, w_ref[:, 2048:2560]), kg_ref, 1.0)
    dv_ref[0] = _dot(h, w_ref[:, 2560:3072]).astype(BF16)


def _mix_in_call(x, g1, w, wup, bgk, qg, kg, ng):
    b, s, d = x.shape
    tm = TM_MIX
    blk = lambda n: pl.BlockSpec((1, tm, n), lambda i, j: (i, j, 0))
    n_chunks, n_pairs = tm // CHUNK, GLA_HEADS // 2
    return pl.pallas_call(
        _mix_in_kernel,
        grid=(b, s // tm),
        in_specs=[blk(d)] + [_const_spec(a.shape) for a in (g1, w, wup, bgk, qg, kg, ng)],
        out_specs=[blk(512)] * 4,
        out_shape=[jax.ShapeDtypeStruct((b, s, 512), BF16)] * 4,
        scratch_shapes=[pltpu.VMEM((tm, 256), BF16), pltpu.VMEM((tm, 256), BF16),
                        pltpu.VMEM((tm, 512), BF16), pltpu.VMEM((tm, 512), BF16),
                        pltpu.VMEM((tm, 256), F32),
                        pltpu.VMEM((n_pairs, GLA_DV, 2 * GLA_DK), F32),
                        pltpu.VMEM((n_chunks, n_pairs, GLA_DV, 2 * GLA_DK), F32),
                        pltpu.VMEM((n_chunks, n_pairs, GLA_DV, 2 * GLA_DK), BF16)],
        compiler_params=pltpu.CompilerParams(dimension_semantics=("arbitrary", "arbitrary"),
                                             vmem_limit_bytes=VMEM_LIMIT),
        name="mix_in",
    )(x, g1, w, wup, bgk, qg, kg, ng)


def _attn_kernel(flag_ref, lam_ref, sg_ref, q_ref, k_ref, v_ref, o_ref, *scratch, lambda_init):
    nq = q_ref.shape[1] // TQ

    def rows(tile):
        return pl.ds(pl.multiple_of(tile * TQ, TQ), TQ)

    def stacked_q(tile):
        q = q_ref[0, rows(tile), :]
        lane = lax.broadcasted_iota(jnp.int32, q.shape, 1)
        zero = jnp.zeros_like(q)
        return jnp.concatenate([jnp.where(lane < DIFF_DH, q, zero),
                                jnp.where(lane >= DIFF_DH, q, zero)], axis=0)

    def scores(qq, kv_tile, masked):
        s = _dot_nt(qq, k_ref[0, rows(kv_tile), :])
        if masked:
            rq = lax.broadcasted_iota(jnp.int32, s.shape, 0) % TQ
            ck = lax.broadcasted_iota(jnp.int32, s.shape, 1)
            s = jnp.where(ck // CHUNK <= rq // CHUNK, s, NEG)
        return s

    def finish(tile, acc, l):
        lamv = lam_ref[...]
        lam = (jnp.exp(jnp.sum(lamv[0:1] * lamv[1:2], axis=-1, keepdims=True))
               - jnp.exp(jnp.sum(lamv[2:3] * lamv[3:4], axis=-1, keepdims=True)) + lambda_init)
        o = acc / l
        o = o[:TQ] - lam * o[TQ:]
        o = o * lax.rsqrt(jnp.mean(o * o, axis=-1, keepdims=True) + EPS) * sg_ref[...]
        o_ref[0, rows(tile), :] = (o * (1.0 - lambda_init)).astype(BF16)

    @pl.when(flag_ref[0] == 1)
    def _():
        def pair(t, qq_ref, acc_ref, l_ref):
            def block(w, kv_tile, masked):
                p = jnp.exp2(scores(qq_ref[w], kv_tile, masked))
                l_ref[w] += sum(p[:, i * LANES:(i + 1) * LANES] for i in range(TQ // LANES))
                acc_ref[w] += _dot(p.astype(BF16), v_ref[0, rows(kv_tile), :])

            tiles = (t, nq - 1 - t)
            for w in range(2):
                qq_ref[w] = stacked_q(tiles[w])
            acc_ref[...] = jnp.zeros_like(acc_ref)
            l_ref[...] = jnp.zeros_like(l_ref)
            for j in range(nq - 1):
                w = jnp.where(j >= t, 1, 0)
                block(w, j - w * t, False)
            for w in range(2):
                block(w, tiles[w], True)
            for w in range(2):
                finish(tiles[w], acc_ref[w], jnp.sum(l_ref[w], axis=-1, keepdims=True))

        group = ATTN_PAIRS_PER_STEP if (nq // 2) % ATTN_PAIRS_PER_STEP == 0 else 1

        def body(i, _):
            for g in range(group):
                pair(i * group + g, *scratch[3 * g:3 * g + 3])
            return 0

        lax.fori_loop(0, nq // 2 // group, body, 0)

    @pl.when(flag_ref[0] == 0)
    def _():
        def tile_body(t, _):
            qq = stacked_q(t)

            def step(kv_tile, carry, masked):
                m, l, acc = carry
                s = scores(qq, kv_tile, masked)
                m_new = jnp.maximum(m, jnp.max(s, axis=-1, keepdims=True))
                p = jnp.exp2(s - m_new)
                alpha = jnp.exp2(m - m_new)
                l = alpha * l + jnp.sum(p, axis=-1, keepdims=True)
                acc = alpha * acc + _dot(p.astype(BF16), v_ref[0, rows(kv_tile), :])
                return m_new, l, acc

            init = (jnp.full((2 * TQ, 1), NEG, F32), jnp.zeros((2 * TQ, 1), F32),
                    jnp.zeros((2 * TQ, DIFF_DV), F32))
            carry = lax.fori_loop(0, t, lambda j, c: step(j, c, False), init)
            _, l, acc = step(t, carry, True)
            finish(t, acc, l)
            return 0

        lax.fori_loop(0, nq, tile_body, 0)


def _attn_call(flag, lamv, sg, dq, dk, dv, lambda_init):
    b, s, _ = dq.shape
    assert (s // TQ) % 2 == 0
    blk = pl.BlockSpec((1, s, LANES), lambda i, h, _: (i, 0, h))
    return pl.pallas_call(
        functools.partial(_attn_kernel, lambda_init=lambda_init),
        grid_spec=pltpu.PrefetchScalarGridSpec(
            num_scalar_prefetch=1,
            grid=(b, DIFF_HEADS),
            in_specs=[_const_spec(lamv.shape), _const_spec(sg.shape), blk, blk, blk],
            out_specs=blk,
            scratch_shapes=[pltpu.VMEM((2, 2 * TQ, LANES), BF16),
                            pltpu.VMEM((2, 2 * TQ, DIFF_DV), F32),
                            pltpu.VMEM((2, 2 * TQ, LANES), F32)] * ATTN_PAIRS_PER_STEP),
        out_shape=jax.ShapeDtypeStruct((b, s, DIFF_HEADS * DIFF_DV), BF16),
        compiler_params=pltpu.CompilerParams(
            dimension_semantics=("arbitrary", "arbitrary"),
            vmem_limit_bytes=VMEM_LIMIT),
        name="attn",
    )(flag, lamv, sg, dq, dk, dv)


def _ffn_kernel(x_ref, og_ref, od_ref, wo_ref, g2_ref, wup_ref, cw_ref, cb_ref, wdn_ref,
                out_ref, carry_ref, h2_ref, acc_ref, u0_ref, u1_ref):
    tm = TM_FFN
    nf = wup_ref.shape[0]
    assert nf % 2 == 1

    @pl.when(pl.program_id(1) == 0)
    def _():
        carry_ref[...] = jnp.zeros_like(carry_ref)

    half = wo_ref.shape[0] // 2
    x1 = x_ref[0] + _dot(og_ref[0], wo_ref[0:half, :]) + _dot(od_ref[0], wo_ref[half:, :])
    ms = jnp.mean(x1 * x1, axis=-1, keepdims=True)
    h2_ref[...] = (x1 * lax.rsqrt(ms + EPS) * g2_ref[...]).astype(BF16)
    acc_ref[...] = x1

    def up(f, u_ref):
        u = _dot(h2_ref[...], wup_ref[f])
        u_ref[0:SUBLANES, :] = carry_ref[f]
        u_ref[SUBLANES:SUBLANES + tm, :] = u
        carry_ref[f] = u[tm - SUBLANES:tm, :]

    def act(f, u_ref):
        cw = cw_ref[f]
        y = (cw[0:1] * u_ref[SUBLANES - 2:SUBLANES - 2 + tm, :]
             + cw[1:2] * u_ref[SUBLANES - 1:SUBLANES - 1 + tm, :]
             + cw[2:3] * u_ref[SUBLANES:SUBLANES + tm, :] + cb_ref[f])
        gate = y[:, :FC]
        return (gate * jax.nn.sigmoid(gate) * y[:, FC:]).astype(BF16)

    def down(f, u_ref):
        acc_ref[...] += _dot(act(f, u_ref), wdn_ref[f])

    up(0, u0_ref)

    def body(i, _):
        f = 2 * i
        up(f + 1, u1_ref)
        down(f, u0_ref)
        up(f + 2, u0_ref)
        down(f + 1, u1_ref)
        return 0

    lax.fori_loop(0, nf // 2, body, 0)
    down(nf - 1, u0_ref)
    out_ref[0] = acc_ref[...]


def _ffn_call(x, og, od, wo, g2, wup, cw, cb, wdn):
    b, s, d = x.shape
    tm = TM_FFN
    nf = wup.shape[0]
    blk = lambda n: pl.BlockSpec((1, tm, n), lambda i, j: (i, j, 0))
    return pl.pallas_call(
        _ffn_kernel,
        grid=(b, s // tm),
        in_specs=[blk(d), blk(512), blk(512), _const_spec(wo.shape), _const_spec(g2.shape),
                  _const_spec(wup.shape), _const_spec(cw.shape), _const_spec(cb.shape),
                  _const_spec(wdn.shape)],
        out_specs=blk(d),
        out_shape=jax.ShapeDtypeStruct((b, s, d), F32),
        scratch_shapes=[pltpu.VMEM((nf, SUBLANES, 2 * FC), F32),
                        pltpu.VMEM((tm, d), BF16),
                        pltpu.VMEM((tm, d), F32),
                        pltpu.VMEM((tm + SUBLANES, 2 * FC), F32),
                        pltpu.VMEM((tm + SUBLANES, 2 * FC), F32)],
        compiler_params=pltpu.CompilerParams(dimension_semantics=("arbitrary", "arbitrary"),
                                             vmem_limit_bytes=VMEM_LIMIT),
        name="ffn",
    )(x, og, od, wo, g2, wup, cw, cb, wdn)


def _prep_layer(l, norm1_g, w_in, gla_w_gk_up, gla_b_gk, gla_norm_g, diff_q_norm_g, diff_k_norm_g,
                diff_lam_q1, diff_lam_k1, diff_lam_q2, diff_lam_k2, diff_sub_g, w_out, norm2_g,
                ffn_w_up, ffn_conv_w, ffn_conv_b, ffn_w_down):
    w = w_in[l]
    d = w.shape[0]
    low0 = 1536
    low = jnp.pad(w[:, low0:low0 + GLA_RANK], ((0, 0), (0, LANES - GLA_RANK)))
    w_r = jnp.concatenate([w[:, :low0], w[:, low0 + GLA_RANK:], low], axis=1).astype(BF16)
    wup_gate = jnp.pad(gla_w_gk_up[l], ((0, LANES - GLA_RANK), (0, 0))).astype(BF16)
    d_ff = ffn_w_down.shape[1]
    nf = d_ff // FC
    wu = ffn_w_up[l]
    wup = jnp.concatenate([wu[:, :d_ff].reshape(d, nf, FC), wu[:, d_ff:].reshape(d, nf, FC)],
                          axis=2).transpose(1, 0, 2).astype(BF16)
    cw = ffn_conv_w[l]
    cw_r = jnp.concatenate([cw[:, :d_ff].reshape(CONV_W, nf, FC), cw[:, d_ff:].reshape(CONV_W, nf, FC)],
                           axis=2).transpose(1, 0, 2)
    cb = ffn_conv_b[l]
    cb_r = jnp.concatenate([cb[:d_ff].reshape(nf, 1, FC), cb[d_ff:].reshape(nf, 1, FC)], axis=2)
    return dict(
        g1=norm1_g[l][None, :], w=w_r, wup_gate=wup_gate, bgk=gla_b_gk[l][None, :],
        qg=jnp.tile(diff_q_norm_g[l], 2 * DIFF_HEADS)[None, :],
        kg=jnp.tile(diff_k_norm_g[l], 2 * DIFF_HEADS)[None, :],
        ng=gla_norm_g[l][None, :],
        lamv=jnp.stack([diff_lam_q1[l], diff_lam_k1[l], diff_lam_q2[l], diff_lam_k2[l]]),
        fast=(1.02 * DIFF_DH ** 0.5 * jnp.max(jnp.abs(diff_q_norm_g[l])) * jnp.max(jnp.abs(diff_k_norm_g[l]))
              <= FAST_SCORE_BOUND).astype(jnp.int32).reshape(1),
        sg=diff_sub_g[l][None, :],
        wo=w_out[l].astype(BF16), g2=norm2_g[l][None, :],
        wup=wup, cw=cw_r, cb=cb_r, wdn=ffn_w_down[l].reshape(nf, FC, d).astype(BF16))


def kernel(x, norm1_g, w_in, gla_w_gk_up, gla_b_gk, gla_norm_g, diff_q_norm_g, diff_k_norm_g,
           diff_lam_q1, diff_lam_k1, diff_lam_q2, diff_lam_k2, diff_sub_g, w_out, norm2_g,
           ffn_w_up, ffn_conv_w, ffn_conv_b, ffn_w_down):
    b, s, d = x.shape
    depth = w_in.shape[0]
    assert s % TM_MIX == 0 and s % TQ == 0 and s % TM_FFN == 0
    assert ffn_w_down.shape[1] % FC == 0
    params = (norm1_g, w_in, gla_w_gk_up, gla_b_gk, gla_norm_g, diff_q_norm_g, diff_k_norm_g,
              diff_lam_q1, diff_lam_k1, diff_lam_q2, diff_lam_k2, diff_sub_g, w_out, norm2_g,
              ffn_w_up, ffn_conv_w, ffn_conv_b, ffn_w_down)
    for l in range(depth):
        p = _prep_layer(l, *params)
        lambda_init = 0.8 - 0.6 * math.exp(-0.3 * l)
        og, dq, dk, dv = _mix_in_call(x, p["g1"], p["w"], p["wup_gate"], p["bgk"], p["qg"], p["kg"], p["ng"])
        od = _attn_call(p["fast"], p["lamv"], p["sg"], dq, dk, dv, lambda_init)
        x = _ffn_call(x, og, od, p["wo"], p["g2"], p["wup"], p["cw"], p["cb"], p["wdn"])
    return x
```

```python
import functools
import math

import jax
import jax.numpy as jnp
from jax import lax
from jax.experimental import pallas as pl
from jax.experimental.pallas import tpu as pltpu

F32 = jnp.float32
BF16 = jnp.bfloat16

EPS = 1e-6
CHUNK = 64
GLA_HEADS = 4
GLA_DK = 64
GLA_DV = 128
GLA_RANK = 16
GLA_TAU = 16.0
DIFF_HEADS = 4
DIFF_DH = 64
DIFF_DV = 128
CONV_W = 3

LANES = 128
SUBLANES = 8
VMEM_LIMIT = 56 * 1024 * 1024

TM_MIX = 512
TQ = 256
TM_FFN = 512
FC = 256
ATTN_PAIRS_PER_STEP = 2
PIECE = 256

NEG = -1e30
LOG2E = math.log2(math.e)
FAST_SCORE_BOUND = 40.0


def _dot(a, b):
    return jnp.dot(a, b, preferred_element_type=F32)


def _dot_nt(a, b):
    return lax.dot_general(a, b, (((1,), (1,)), ((), ())), preferred_element_type=F32)


def _dot_tn(a, b):
    return lax.dot_general(a, b, (((0,), (0,)), ((), ())), preferred_element_type=F32)


def _const_spec(shape):
    nd = len(shape)
    return pl.BlockSpec(shape, lambda *_: (0,) * nd, pipeline_mode=pl.Buffered(1))


def _mix_in_kernel(x_ref, g1_ref, w_ref, wup_ref, bgk_ref, qg_ref, kg_ref, ng_ref,
                   og_ref, dq_ref, dk_ref, dv_ref,
                   q_ref, k_ref, v_ref, r_ref, la_ref, st_ref, kvt_ref, stb_ref):
    @pl.when(pl.program_id(1) == 0)
    def _():
        st_ref[...] = jnp.zeros_like(st_ref)

    x = x_ref[0]
    ms = jnp.mean(x * x, axis=-1, keepdims=True)
    h = (x * lax.rsqrt(ms + EPS) * g1_ref[...]).astype(BF16)

    def qk_norm(p, g, scale):
        lo = lax.broadcasted_iota(jnp.int32, (p.shape[0], LANES), 1) < DIFF_DH
        blocks = []
        for j in range(p.shape[1] // LANES):
            blk = p[:, j * LANES:(j + 1) * LANES]
            sq = blk * blk
            s_lo = jnp.sum(jnp.where(lo, sq, 0.0), axis=-1, keepdims=True)
            s_hi = jnp.sum(jnp.where(lo, 0.0, sq), axis=-1, keepdims=True)
            blocks.append(blk * lax.rsqrt(jnp.where(lo, s_lo, s_hi) * (1.0 / DIFF_DH) + EPS))
        return (jnp.concatenate(blocks, axis=-1) * (g * scale)).astype(BF16)

    k_ref[...] = _dot(h, w_ref[:, 256:512]).astype(BF16)
    v_ref[...] = _dot(h, w_ref[:, 512:1024]).astype(BF16)

    low = _dot(h, w_ref[:, 3072:3200]).astype(BF16)
    z = _dot(low, wup_ref[...]) + bgk_ref[...]
    logsig = jnp.minimum(z, 0.0) - jnp.log1p(jnp.exp(-jnp.abs(z)))
    la_ref[...] = logsig * (1.0 / GLA_TAU)

    ri = lax.broadcasted_iota(jnp.int32, (CHUNK, CHUNK), 0)
    ci = lax.broadcasted_iota(jnp.int32, (CHUNK, CHUNK), 1)
    tril = jnp.where(ci <= ri, 1.0, 0.0).astype(BF16)
    lane = lax.broadcasted_iota(jnp.int32, (CHUNK, LANES), 1)
    head_mask = (lane < GLA_DK, lane >= GLA_DK)
    ng = ng_ref[...]
    n_chunks = TM_MIX // CHUNK
    n_pairs = GLA_HEADS // 2
    chunk_rows = lambda c: slice(c * CHUNK, (c + 1) * CHUNK)
    pair_lanes = lambda p: slice(p * LANES, (p + 1) * LANES)

    def piece(i):
        if i == 0:
            q_ref[...] = (_dot(h, w_ref[:, 0:PIECE]) * (GLA_DK ** -0.5)).astype(BF16)
            return
        seg, half = divmod(i - 1, 2)
        cols = slice(half * PIECE, (half + 1) * PIECE)
        w0 = 1024 + seg * 512 + half * PIECE
        p = _dot(h, w_ref[:, w0:w0 + PIECE])
        if seg == 0:
            r_ref[:, cols] = p.astype(BF16)
        elif seg == 1:
            dq_ref[0, :, cols] = qk_norm(p, qg_ref[:, cols], DIFF_DH ** -0.5 * LOG2E)
        elif seg == 2:
            dk_ref[0, :, cols] = qk_norm(p, kg_ref[:, cols], 1.0)
        else:
            dv_ref[0, :, cols] = p.astype(BF16)

    chunks = range(n_chunks)
    heads = [(pair, hh) for pair in range(n_pairs) for hh in range(2)]

    cums = []
    for c in chunks:
        la = la_ref[chunk_rows(c), :]
        la_hi = la.astype(BF16)
        la_lo = (la - la_hi.astype(F32)).astype(BF16)
        cums.append(_dot(tril, la_hi) + _dot(tril, la_lo))
    piece(0)
    piece(1)

    kdecs, decays = [], []
    for c in chunks:
        total = cums[c][CHUNK - 1:CHUNK, :]
        kdecs.append(k_ref[chunk_rows(c), :].astype(F32) * jnp.exp(total - cums[c]))
        decays.append(jnp.exp(total))
    piece(2)

    for c in chunks:
        for pair in range(n_pairs):
            kp = kdecs[c][:, pair_lanes(pair)]
            kvt = None
            for hh in range(2):
                head = 2 * pair + hh
                kh = jnp.where(head_mask[hh], kp, 0.0).astype(BF16)
                t = _dot_tn(v_ref[chunk_rows(c), head * GLA_DV:(head + 1) * GLA_DV], kh)
                kvt = t if kvt is None else kvt + t
            kvt_ref[c, pair] = kvt
    piece(3)
    piece(4)

    for pair in range(n_pairs):
        st = st_ref[pair]
        for c in chunks:
            st = st * decays[c][:, pair_lanes(pair)] + kvt_ref[c, pair]
            stb_ref[c, pair] = st.astype(BF16)
        st_ref[pair] = st
    piece(5)

    outs = {}
    for c in chunks:
        q = q_ref[chunk_rows(c), :]
        for pair, hh in heads:
            qp = q[:, pair_lanes(pair)]
            qh = jnp.where(head_mask[hh], qp, jnp.zeros_like(qp))
            outs[c, pair, hh] = _dot_nt(qh, stb_ref[c, pair])
    piece(6)
    piece(7)

    for c in chunks:
        for pair, hh in heads:
            cols = slice((2 * pair + hh) * GLA_DV, (2 * pair + hh + 1) * GLA_DV)
            o = outs[c, pair, hh]
            o = o * lax.rsqrt(jnp.mean(o * o, axis=-1, keepdims=True) + EPS) * ng
            r = r_ref[chunk_rows(c), cols].astype(F32)
            og_ref[0, chunk_rows(c), cols] = (o * (r * jax.nn.sigmoid(r))).astype(BF16)
    piece(8)


def _mix_in_call(x, g1, w, wup, bgk, qg, kg, ng):
    b, s, d = x.shape
    tm = TM_MIX
    blk = lambda n: pl.BlockSpec((1, tm, n), lambda i, j: (i, j, 0))
    n_chunks, n_pairs = tm // CHUNK, GLA_HEADS // 2
    return pl.pallas_call(
        _mix_in_kernel,
        grid=(b, s // tm),
        in_specs=[blk(d)] + [_const_spec(a.shape) for a in (g1, w, wup, bgk, qg, kg, ng)],
        out_specs=[blk(512)] * 4,
        out_shape=[jax.ShapeDtypeStruct((b, s, 512), BF16)] * 4,
        scratch_shapes=[pltpu.VMEM((tm, 256), BF16), pltpu.VMEM((tm, 256), BF16),
                        pltpu.VMEM((tm, 512), BF16), pltpu.VMEM((tm, 512), BF16),
                        pltpu.VMEM((tm, 256), F32),
                        pltpu.VMEM((n_pairs, GLA_DV, 2 * GLA_DK), F32),
                        pltpu.VMEM((n_chunks, n_pairs, GLA_DV, 2 * GLA_DK), F32),
                        pltpu.VMEM((n_chunks, n_pairs, GLA_DV, 2 * GLA_DK), BF16)],
        compiler_params=pltpu.CompilerParams(dimension_semantics=("arbitrary", "arbitrary"),
                                             vmem_limit_bytes=VMEM_LIMIT),
        name="mix_in",
    )(x, g1, w, wup, bgk, qg, kg, ng)


def _attn_kernel(flag_ref, lam_ref, sg_ref, q_ref, k_ref, v_ref, o_ref, *scratch, lambda_init):
    nq = q_ref.shape[1] // TQ

    def rows(tile):
        return pl.ds(pl.multiple_of(tile * TQ, TQ), TQ)

    def stacked_q(tile):
        q = q_ref[0, rows(tile), :]
        lane = lax.broadcasted_iota(jnp.int32, q.shape, 1)
        zero = jnp.zeros_like(q)
        return jnp.concatenate([jnp.where(lane < DIFF_DH, q, zero),
                                jnp.where(lane >= DIFF_DH, q, zero)], axis=0)

    def scores(qq, kv_tile, masked):
        s = _dot_nt(qq, k_ref[0, rows(kv_tile), :])
        if masked:
            rq = lax.broadcasted_iota(jnp.int32, s.shape, 0) % TQ
            ck = lax.broadcasted_iota(jnp.int32, s.shape, 1)
            s = jnp.where(ck // CHUNK <= rq // CHUNK, s, NEG)
        return s

    def finish(tile, acc, l):
        lamv = lam_ref[...]
        lam = (jnp.exp(jnp.sum(lamv[0:1] * lamv[1:2], axis=-1, keepdims=True))
               - jnp.exp(jnp.sum(lamv[2:3] * lamv[3:4], axis=-1, keepdims=True)) + lambda_init)
        o = acc / l
        o = o[:TQ] - lam * o[TQ:]
        o = o * lax.rsqrt(jnp.mean(o * o, axis=-1, keepdims=True) + EPS) * sg_ref[...]
        o_ref[0, rows(tile), :] = (o * (1.0 - lambda_init)).astype(BF16)

    @pl.when(flag_ref[0] == 1)
    def _():
        def pair(t, qq_ref, acc_ref, l_ref):
            def block(w, kv_tile, masked):
                p = jnp.exp2(scores(qq_ref[w], kv_tile, masked))
                l_ref[w] += sum(p[:, i * LANES:(i + 1) * LANES] for i in range(TQ // LANES))
                acc_ref[w] += _dot(p.astype(BF16), v_ref[0, rows(kv_tile), :])

            tiles = (t, nq - 1 - t)
            for w in range(2):
                qq_ref[w] = stacked_q(tiles[w])
            acc_ref[...] = jnp.zeros_like(acc_ref)
            l_ref[...] = jnp.zeros_like(l_ref)
            for j in range(nq - 1):
                w = jnp.where(j >= t, 1, 0)
                block(w, j - w * t, False)
            for w in range(2):
                block(w, tiles[w], True)
            for w in range(2):
                finish(tiles[w], acc_ref[w], jnp.sum(l_ref[w], axis=-1, keepdims=True))

        group = ATTN_PAIRS_PER_STEP if (nq // 2) % ATTN_PAIRS_PER_STEP == 0 else 1

        def body(i, _):
            for g in range(group):
                pair(i * group + g, *scratch[3 * g:3 * g + 3])
            return 0

        lax.fori_loop(0, nq // 2 // group, body, 0)

    @pl.when(flag_ref[0] == 0)
    def _():
        def tile_body(t, _):
            qq = stacked_q(t)

            def step(kv_tile, carry, masked):
                m, l, acc = carry
                s = scores(qq, kv_tile, masked)
                m_new = jnp.maximum(m, jnp.max(s, axis=-1, keepdims=True))
                p = jnp.exp2(s - m_new)
                alpha = jnp.exp2(m - m_new)
                l = alpha * l + jnp.sum(p, axis=-1, keepdims=True)
                acc = alpha * acc + _dot(p.astype(BF16), v_ref[0, rows(kv_tile), :])
                return m_new, l, acc

            init = (jnp.full((2 * TQ, 1), NEG, F32), jnp.zeros((2 * TQ, 1), F32),
                    jnp.zeros((2 * TQ, DIFF_DV), F32))
            carry = lax.fori_loop(0, t, lambda j, c: step(j, c, False), init)
            _, l, acc = step(t, carry, True)
            finish(t, acc, l)
            return 0

        lax.fori_loop(0, nq, tile_body, 0)


def _attn_call(flag, lamv, sg, dq, dk, dv, lambda_init):
    b, s, _ = dq.shape
    assert (s // TQ) % 2 == 0
    blk = pl.BlockSpec((1, s, LANES), lambda i, h, _: (i, 0, h))
    return pl.pallas_call(
        functools.partial(_attn_kernel, lambda_init=lambda_init),
        grid_spec=pltpu.PrefetchScalarGridSpec(
            num_scalar_prefetch=1,
            grid=(b, DIFF_HEADS),
            in_specs=[_const_spec(lamv.shape), _const_spec(sg.shape), blk, blk, blk],
            out_specs=blk,
            scratch_shapes=[pltpu.VMEM((2, 2 * TQ, LANES), BF16),
                            pltpu.VMEM((2, 2 * TQ, DIFF_DV), F32),
                            pltpu.VMEM((2, 2 * TQ, LANES), F32)] * ATTN_PAIRS_PER_STEP),
        out_shape=jax.ShapeDtypeStruct((b, s, DIFF_HEADS * DIFF_DV), BF16),
        compiler_params=pltpu.CompilerParams(
            dimension_semantics=("arbitrary", "arbitrary"),
            vmem_limit_bytes=VMEM_LIMIT),
        name="attn",
    )(flag, lamv, sg, dq, dk, dv)


def _ffn_kernel(x_ref, og_ref, od_ref, wo_ref, g2_ref, wup_ref, cw_ref, cb_ref, wdn_ref,
                out_ref, carry_ref, h2_ref, acc_ref, u0_ref, u1_ref):
    tm = TM_FFN
    nf = wup_ref.shape[0]
    assert nf % 2 == 1

    @pl.when(pl.program_id(1) == 0)
    def _():
        carry_ref[...] = jnp.zeros_like(carry_ref)

    half = wo_ref.shape[0] // 2
    x1 = x_ref[0] + _dot(og_ref[0], wo_ref[0:half, :]) + _dot(od_ref[0], wo_ref[half:, :])
    ms = jnp.mean(x1 * x1, axis=-1, keepdims=True)
    h2_ref[...] = (x1 * lax.rsqrt(ms + EPS) * g2_ref[...]).astype(BF16)
    acc_ref[...] = x1

    def up(f, u_ref):
        u = _dot(h2_ref[...], wup_ref[f])
        u_ref[0:SUBLANES, :] = carry_ref[f]
        u_ref[SUBLANES:SUBLANES + tm, :] = u
        carry_ref[f] = u[tm - SUBLANES:tm, :]

    def act(f, u_ref):
        cw = cw_ref[f]
        y = (cw[0:1] * u_ref[SUBLANES - 2:SUBLANES - 2 + tm, :]
             + cw[1:2] * u_ref[SUBLANES - 1:SUBLANES - 1 + tm, :]
             + cw[2:3] * u_ref[SUBLANES:SUBLANES + tm, :] + cb_ref[f])
        gate = y[:, :FC]
        return (gate * jax.nn.sigmoid(gate) * y[:, FC:]).astype(BF16)

    def down(f, u_ref):
        acc_ref[...] += _dot(act(f, u_ref), wdn_ref[f])

    up(0, u0_ref)

    def body(i, _):
        f = 2 * i
        up(f + 1, u1_ref)
        down(f, u0_ref)
        up(f + 2, u0_ref)
        down(f + 1, u1_ref)
        return 0

    lax.fori_loop(0, nf // 2, body, 0)
    down(nf - 1, u0_ref)
    out_ref[0] = acc_ref[...]


def _ffn_call(x, og, od, wo, g2, wup, cw, cb, wdn):
    b, s, d = x.shape
    tm = TM_FFN
    nf = wup.shape[0]
    blk = lambda n: pl.BlockSpec((1, tm, n), lambda i, j: (i, j, 0))
    return pl.pallas_call(
        _ffn_kernel,
        grid=(b, s // tm),
        in_specs=[blk(d), blk(512), blk(512), _const_spec(wo.shape), _const_spec(g2.shape),
                  _const_spec(wup.shape), _const_spec(cw.shape), _const_spec(cb.shape),
                  _const_spec(wdn.shape)],
        out_specs=blk(d),
        out_shape=jax.ShapeDtypeStruct((b, s, d), F32),
        scratch_shapes=[pltpu.VMEM((nf, SUBLANES, 2 * FC), F32),
                        pltpu.VMEM((tm, d), BF16),
                        pltpu.VMEM((tm, d), F32),
                        pltpu.VMEM((tm + SUBLANES, 2 * FC), F32),
                        pltpu.VMEM((tm + SUBLANES, 2 * FC), F32)],
        compiler_params=pltpu.CompilerParams(dimension_semantics=("arbitrary", "arbitrary"),
                                             vmem_limit_bytes=VMEM_LIMIT),
        name="ffn",
    )(x, og, od, wo, g2, wup, cw, cb, wdn)


def _prep_layer(l, norm1_g, w_in, gla_w_gk_up, gla_b_gk, gla_norm_g, diff_q_norm_g, diff_k_norm_g,
                diff_lam_q1, diff_lam_k1, diff_lam_q2, diff_lam_k2, diff_sub_g, w_out, norm2_g,
                ffn_w_up, ffn_conv_w, ffn_conv_b, ffn_w_down):
    w = w_in[l]
    d = w.shape[0]
    low0 = 1536
    low = jnp.pad(w[:, low0:low0 + GLA_RANK], ((0, 0), (0, LANES - GLA_RANK)))
    w_r = jnp.concatenate([w[:, :low0], w[:, low0 + GLA_RANK:], low], axis=1).astype(BF16)
    wup_gate = jnp.pad(gla_w_gk_up[l], ((0, LANES - GLA_RANK), (0, 0))).astype(BF16)
    d_ff = ffn_w_down.shape[1]
    nf = d_ff // FC
    wu = ffn_w_up[l]
    wup = jnp.concatenate([wu[:, :d_ff].reshape(d, nf, FC), wu[:, d_ff:].reshape(d, nf, FC)],
                          axis=2).transpose(1, 0, 2).astype(BF16)
    cw = ffn_conv_w[l]
    cw_r = jnp.concatenate([cw[:, :d_ff].reshape(CONV_W, nf, FC), cw[:, d_ff:].reshape(CONV_W, nf, FC)],
                           axis=2).transpose(1, 0, 2)
    cb = ffn_conv_b[l]
    cb_r = jnp.concatenate([cb[:d_ff].reshape(nf, 1, FC), cb[d_ff:].reshape(nf, 1, FC)], axis=2)
    return dict(
        g1=norm1_g[l][None, :], w=w_r, wup_gate=wup_gate, bgk=gla_b_gk[l][None, :],
        qg=jnp.tile(diff_q_norm_g[l], 2 * DIFF_HEADS)[None, :],
        kg=jnp.tile(diff_k_norm_g[l], 2 * DIFF_HEADS)[None, :],
        ng=gla_norm_g[l][None, :],
        lamv=jnp.stack([diff_lam_q1[l], diff_lam_k1[l], diff_lam_q2[l], diff_lam_k2[l]]),
        fast=(1.02 * DIFF_DH ** 0.5 * jnp.max(jnp.abs(diff_q_norm_g[l])) * jnp.max(jnp.abs(diff_k_norm_g[l]))
              <= FAST_SCORE_BOUND).astype(jnp.int32).reshape(1),
        sg=diff_sub_g[l][None, :],
        wo=w_out[l].astype(BF16), g2=norm2_g[l][None, :],
        wup=wup, cw=cw_r, cb=cb_r, wdn=ffn_w_down[l].reshape(nf, FC, d).astype(BF16))


def kernel(x, norm1_g, w_in, gla_w_gk_up, gla_b_gk, gla_norm_g, diff_q_norm_g, diff_k_norm_g,
           diff_lam_q1, diff_lam_k1, diff_lam_q2, diff_lam_k2, diff_sub_g, w_out, norm2_g,
           ffn_w_up, ffn_conv_w, ffn_conv_b, ffn_w_down):
    b, s, d = x.shape
    depth = w_in.shape[0]
    assert s % TM_MIX == 0 and s % TQ == 0 and s % TM_FFN == 0
    assert ffn_w_down.shape[1] % FC == 0
    params = (norm1_g, w_in, gla_w_gk_up, gla_b_gk, gla_norm_g, diff_q_norm_g, diff_k_norm_g,
              diff_lam_q1, diff_lam_k1, diff_lam_q2, diff_lam_k2, diff_sub_g, w_out, norm2_g,
              ffn_w_up, ffn_conv_w, ffn_conv_b, ffn_w_down)
    for l in range(depth):
        p = _prep_layer(l, *params)
        lambda_init = 0.8 - 0.6 * math.exp(-0.3 * l)
        og, dq, dk, dv = _mix_in_call(x, p["g1"], p["w"], p["wup_gate"], p["bgk"], p["qg"], p["kg"], p["ng"])
        od = _attn_call(p["fast"], p["lamv"], p["sg"], dq, dk, dv, lambda_init)
        x = _ffn_call(x, og, od, p["wo"], p["g2"], p["wup"], p["cw"], p["cb"], p["wdn"])
    return x
```

```python
import functools
import math

import jax
import jax.numpy as jnp
from jax import lax
from jax.experimental import pallas as pl
from jax.experimental.pallas import tpu as pltpu

F32 = jnp.float32
BF16 = jnp.bfloat16

EPS = 1e-6
CHUNK = 64
GLA_HEADS = 4
GLA_DK = 64
GLA_DV = 128
GLA_RANK = 16
GLA_TAU = 16.0
DIFF_HEADS = 4
DIFF_DH = 64
DIFF_DV = 128
CONV_W = 3

LANES = 128
SUBLANES = 8
VMEM_LIMIT = 56 * 1024 * 1024

TM_MIX = 1024
TQ = 256
TM_FFN = 512
FC = 256
ATTN_PAIRS_PER_STEP = 4
PIECE = 256

NEG = -1e30
LOG2E = math.log2(math.e)
FAST_SCORE_BOUND = 40.0


def _dot(a, b):
    return jnp.dot(a, b, preferred_element_type=F32)


def _dot_nt(a, b):
    return lax.dot_general(a, b, (((1,), (1,)), ((), ())), preferred_element_type=F32)


def _dot_tn(a, b):
    return lax.dot_general(a, b, (((0,), (0,)), ((), ())), preferred_element_type=F32)


def _const_spec(shape):
    nd = len(shape)
    return pl.BlockSpec(shape, lambda *_: (0,) * nd, pipeline_mode=pl.Buffered(1))


def _mix_in_kernel(x_ref, g1_ref, w_ref, wup_ref, bgk_ref, qg_ref, kg_ref, ng_ref,
                   og_ref, dq_ref, dk_ref, dv_ref,
                   q_ref, k_ref, v_ref, r_ref, la_ref, st_ref, kvt_ref, stb_ref):
    @pl.when(pl.program_id(1) == 0)
    def _():
        st_ref[...] = jnp.zeros_like(st_ref)

    x = x_ref[0]
    ms = jnp.mean(x * x, axis=-1, keepdims=True)
    h = (x * lax.rsqrt(ms + EPS) * g1_ref[...]).astype(BF16)

    def qk_norm(p, g, scale):
        lo = lax.broadcasted_iota(jnp.int32, (p.shape[0], LANES), 1) < DIFF_DH
        blocks = []
        for j in range(p.shape[1] // LANES):
            blk = p[:, j * LANES:(j + 1) * LANES]
            sq = blk * blk
            s_lo = jnp.sum(jnp.where(lo, sq, 0.0), axis=-1, keepdims=True)
            s_hi = jnp.sum(jnp.where(lo, 0.0, sq), axis=-1, keepdims=True)
            blocks.append(blk * lax.rsqrt(jnp.where(lo, s_lo, s_hi) * (1.0 / DIFF_DH) + EPS))
        return (jnp.concatenate(blocks, axis=-1) * (g * scale)).astype(BF16)

    k_ref[...] = _dot(h, w_ref[:, 256:512]).astype(BF16)
    v_ref[...] = _dot(h, w_ref[:, 512:1024]).astype(BF16)

    low = _dot(h, w_ref[:, 3072:3200]).astype(BF16)
    z = _dot(low, wup_ref[...]) + bgk_ref[...]
    logsig = jnp.minimum(z, 0.0) - jnp.log1p(jnp.exp(-jnp.abs(z)))
    la_ref[...] = logsig * (1.0 / GLA_TAU)

    ri = lax.broadcasted_iota(jnp.int32, (CHUNK, CHUNK), 0)
    ci = lax.broadcasted_iota(jnp.int32, (CHUNK, CHUNK), 1)
    tril = jnp.where(ci <= ri, 1.0, 0.0).astype(BF16)
    lane = lax.broadcasted_iota(jnp.int32, (CHUNK, LANES), 1)
    head_mask = (lane < GLA_DK, lane >= GLA_DK)
    ng = ng_ref[...]
    n_chunks = TM_MIX // CHUNK
    n_pairs = GLA_HEADS // 2
    chunk_rows = lambda c: slice(c * CHUNK, (c + 1) * CHUNK)
    pair_lanes = lambda p: slice(p * LANES, (p + 1) * LANES)

    def piece(i):
        if i == 0:
            q_ref[...] = (_dot(h, w_ref[:, 0:PIECE]) * (GLA_DK ** -0.5)).astype(BF16)
            return
        seg, half = divmod(i - 1, 2)
        cols = slice(half * PIECE, (half + 1) * PIECE)
        w0 = 1024 + seg * 512 + half * PIECE
        p = _dot(h, w_ref[:, w0:w0 + PIECE])
        if seg == 0:
            r_ref[:, cols] = p.astype(BF16)
        elif seg == 1:
            dq_ref[0, :, cols] = qk_norm(p, qg_ref[:, cols], DIFF_DH ** -0.5 * LOG2E)
        elif seg == 2:
            dk_ref[0, :, cols] = qk_norm(p, kg_ref[:, cols], 1.0)
        else:
            dv_ref[0, :, cols] = p.astype(BF16)

    chunks = range(n_chunks)
    heads = [(pair, hh) for pair in range(n_pairs) for hh in range(2)]

    cums = []
    for c in chunks:
        la = la_ref[chunk_rows(c), :]
        la_hi = la.astype(BF16)
        la_lo = (la - la_hi.astype(F32)).astype(BF16)
        cums.append(_dot(tril, la_hi) + _dot(tril, la_lo))
    piece(0)
    piece(1)

    kdecs, decays = [], []
    for c in chunks:
        total = cums[c][CHUNK - 1:CHUNK, :]
        kdecs.append(k_ref[chunk_rows(c), :].astype(F32) * jnp.exp(total - cums[c]))
        decays.append(jnp.exp(total))
    piece(2)

    for c in chunks:
        for pair in range(n_pairs):
            kp = kdecs[c][:, pair_lanes(pair)]
            kvt = None
            for hh in range(2):
                head = 2 * pair + hh
                kh = jnp.where(head_mask[hh], kp, 0.0).astype(BF16)
                t = _dot_tn(v_ref[chunk_rows(c), head * GLA_DV:(head + 1) * GLA_DV], kh)
                kvt = t if kvt is None else kvt + t
            kvt_ref[c, pair] = kvt
    piece(3)
    piece(4)

    for pair in range(n_pairs):
        st = st_ref[pair]
        for c in chunks:
            st = st * decays[c][:, pair_lanes(pair)] + kvt_ref[c, pair]
            stb_ref[c, pair] = st.astype(BF16)
        st_ref[pair] = st
    piece(5)

    outs = {}
    for c in chunks:
        q = q_ref[chunk_rows(c), :]
        for pair, hh in heads:
            qp = q[:, pair_lanes(pair)]
            qh = jnp.where(head_mask[hh], qp, jnp.zeros_like(qp))
            outs[c, pair, hh] = _dot_nt(qh, stb_ref[c, pair])
    piece(6)
    piece(7)

    for c in chunks:
        for pair, hh in heads:
            cols = slice((2 * pair + hh) * GLA_DV, (2 * pair + hh + 1) * GLA_DV)
            o = outs[c, pair, hh]
            o = o * lax.rsqrt(jnp.mean(o * o, axis=-1, keepdims=True) + EPS) * ng
            r = r_ref[chunk_rows(c), cols].astype(F32)
            og_ref[0, chunk_rows(c), cols] = (o * (r * jax.nn.sigmoid(r))).astype(BF16)
    piece(8)


def _mix_in_call(x, g1, w, wup, bgk, qg, kg, ng):
    b, s, d = x.shape
    tm = TM_MIX
    blk = lambda n: pl.BlockSpec((1, tm, n), lambda i, j: (i, j, 0))
    n_chunks, n_pairs = tm // CHUNK, GLA_HEADS // 2
    return pl.pallas_call(
        _mix_in_kernel,
        grid=(b, s // tm),
        in_specs=[blk(d)] + [_const_spec(a.shape) for a in (g1, w, wup, bgk, qg, kg, ng)],
        out_specs=[blk(512)] * 4,
        out_shape=[jax.ShapeDtypeStruct((b, s, 512), BF16)] * 4,
        scratch_shapes=[pltpu.VMEM((tm, 256), BF16), pltpu.VMEM((tm, 256), BF16),
                        pltpu.VMEM((tm, 512), BF16), pltpu.VMEM((tm, 512), BF16),
                        pltpu.VMEM((tm, 256), F32),
                        pltpu.VMEM((n_pairs, GLA_DV, 2 * GLA_DK), F32),
                        pltpu.VMEM((n_chunks, n_pairs, GLA_DV, 2 * GLA_DK), F32),
                        pltpu.VMEM((n_chunks, n_pairs, GLA_DV, 2 * GLA_DK), BF16)],
        compiler_params=pltpu.CompilerParams(dimension_semantics=("arbitrary", "arbitrary"),
                                             vmem_limit_bytes=VMEM_LIMIT),
        name="mix_in",
    )(x, g1, w, wup, bgk, qg, kg, ng)


def _attn_kernel(flag_ref, lam_ref, sg_ref, q_ref, k_ref, v_ref, o_ref, *scratch, lambda_init):
    nq = q_ref.shape[1] // TQ

    def rows(tile):
        return pl.ds(pl.multiple_of(tile * TQ, TQ), TQ)

    def stacked_q(tile):
        q = q_ref[0, rows(tile), :]
        lane = lax.broadcasted_iota(jnp.int32, q.shape, 1)
        zero = jnp.zeros_like(q)
        return jnp.concatenate([jnp.where(lane < DIFF_DH, q, zero),
                                jnp.where(lane >= DIFF_DH, q, zero)], axis=0)

    def scores(qq, kv_tile, masked):
        s = _dot_nt(qq, k_ref[0, rows(kv_tile), :])
        if masked:
            rq = lax.broadcasted_iota(jnp.int32, s.shape, 0) % TQ
            ck = lax.broadcasted_iota(jnp.int32, s.shape, 1)
            s = jnp.where(ck // CHUNK <= rq // CHUNK, s, NEG)
        return s

    def finish(tile, acc, l):
        lamv = lam_ref[...]
        lam = (jnp.exp(jnp.sum(lamv[0:1] * lamv[1:2], axis=-1, keepdims=True))
               - jnp.exp(jnp.sum(lamv[2:3] * lamv[3:4], axis=-1, keepdims=True)) + lambda_init)
        o = acc / l
        o = o[:TQ] - lam * o[TQ:]
        o = o * lax.rsqrt(jnp.mean(o * o, axis=-1, keepdims=True) + EPS) * sg_ref[...]
        o_ref[0, rows(tile), :] = (o * (1.0 - lambda_init)).astype(BF16)

    @pl.when(flag_ref[0] == 1)
    def _():
        def pair(t, qq_ref, acc_ref, l_ref):
            def block(w, kv_tile, masked):
                p = jnp.exp2(scores(qq_ref[w], kv_tile, masked))
                l_ref[w] += sum(p[:, i * LANES:(i + 1) * LANES] for i in range(TQ // LANES))
                acc_ref[w] += _dot(p.astype(BF16), v_ref[0, rows(kv_tile), :])

            tiles = (t, nq - 1 - t)
            for w in range(2):
                qq_ref[w] = stacked_q(tiles[w])
            acc_ref[...] = jnp.zeros_like(acc_ref)
            l_ref[...] = jnp.zeros_like(l_ref)
            for j in range(nq - 1):
                w = jnp.where(j >= t, 1, 0)
                block(w, j - w * t, False)
            for w in range(2):
                block(w, tiles[w], True)
            for w in range(2):
                finish(tiles[w], acc_ref[w], jnp.sum(l_ref[w], axis=-1, keepdims=True))

        group = ATTN_PAIRS_PER_STEP if (nq // 2) % ATTN_PAIRS_PER_STEP == 0 else 1

        def body(i, _):
            for g in range(group):
                pair(i * group + g, *scratch[3 * g:3 * g + 3])
            return 0

        lax.fori_loop(0, nq // 2 // group, body, 0)

    @pl.when(flag_ref[0] == 0)
    def _():
        def tile_body(t, _):
            qq = stacked_q(t)

            def step(kv_tile, carry, masked):
                m, l, acc = carry
                s = scores(qq, kv_tile, masked)
                m_new = jnp.maximum(m, jnp.max(s, axis=-1, keepdims=True))
                p = jnp.exp2(s - m_new)
                alpha = jnp.exp2(m - m_new)
                l = alpha * l + jnp.sum(p, axis=-1, keepdims=True)
                acc = alpha * acc + _dot(p.astype(BF16), v_ref[0, rows(kv_tile), :])
                return m_new, l, acc

            init = (jnp.full((2 * TQ, 1), NEG, F32), jnp.zeros((2 * TQ, 1), F32),
                    jnp.zeros((2 * TQ, DIFF_DV), F32))
            carry = lax.fori_loop(0, t, lambda j, c: step(j, c, False), init)
            _, l, acc = step(t, carry, True)
            finish(t, acc, l)
            return 0

        lax.fori_loop(0, nq, tile_body, 0)


def _attn_call(flag, lamv, sg, dq, dk, dv, lambda_init):
    b, s, _ = dq.shape
    assert (s // TQ) % 2 == 0
    blk = pl.BlockSpec((1, s, LANES), lambda i, h, _: (i, 0, h))
    return pl.pallas_call(
        functools.partial(_attn_kernel, lambda_init=lambda_init),
        grid_spec=pltpu.PrefetchScalarGridSpec(
            num_scalar_prefetch=1,
            grid=(b, DIFF_HEADS),
            in_specs=[_const_spec(lamv.shape), _const_spec(sg.shape), blk, blk, blk],
            out_specs=blk,
            scratch_shapes=[pltpu.VMEM((2, 2 * TQ, LANES), BF16),
                            pltpu.VMEM((2, 2 * TQ, DIFF_DV), F32),
                            pltpu.VMEM((2, 2 * TQ, LANES), F32)] * ATTN_PAIRS_PER_STEP),
        out_shape=jax.ShapeDtypeStruct((b, s, DIFF_HEADS * DIFF_DV), BF16),
        compiler_params=pltpu.CompilerParams(
            dimension_semantics=("arbitrary", "arbitrary"),
            vmem_limit_bytes=VMEM_LIMIT),
        name="attn",
    )(flag, lamv, sg, dq, dk, dv)


def _ffn_kernel(x_ref, og_ref, od_ref, wo_ref, g2_ref, wup_ref, cw_ref, cb_ref, wdn_ref,
                out_ref, carry_ref, h2_ref, acc_ref, u0_ref, u1_ref):
    tm = TM_FFN
    nf = wup_ref.shape[0]
    assert nf % 2 == 1

    @pl.when(pl.program_id(1) == 0)
    def _():
        carry_ref[...] = jnp.zeros_like(carry_ref)

    half = wo_ref.shape[0] // 2
    x1 = x_ref[0] + _dot(og_ref[0], wo_ref[0:half, :]) + _dot(od_ref[0], wo_ref[half:, :])
    ms = jnp.mean(x1 * x1, axis=-1, keepdims=True)
    h2_ref[...] = (x1 * lax.rsqrt(ms + EPS) * g2_ref[...]).astype(BF16)
    acc_ref[...] = x1

    def up(f, u_ref):
        u = _dot(h2_ref[...], wup_ref[f])
        u_ref[0:SUBLANES, :] = carry_ref[f]
        u_ref[SUBLANES:SUBLANES + tm, :] = u
        carry_ref[f] = u[tm - SUBLANES:tm, :]

    def act(f, u_ref):
        cw = cw_ref[f]
        y = (cw[0:1] * u_ref[SUBLANES - 2:SUBLANES - 2 + tm, :]
             + cw[1:2] * u_ref[SUBLANES - 1:SUBLANES - 1 + tm, :]
             + cw[2:3] * u_ref[SUBLANES:SUBLANES + tm, :] + cb_ref[f])
        gate = y[:, :FC]
        return (gate * jax.nn.sigmoid(gate) * y[:, FC:]).astype(BF16)

    def down(f, u_ref):
        acc_ref[...] += _dot(act(f, u_ref), wdn_ref[f])

    up(0, u0_ref)

    def body(i, _):
        f = 2 * i
        up(f + 1, u1_ref)
        down(f, u0_ref)
        up(f + 2, u0_ref)
        down(f + 1, u1_ref)
        return 0

    lax.fori_loop(0, nf // 2, body, 0)
    down(nf - 1, u0_ref)
    out_ref[0] = acc_ref[...]


def _ffn_call(x, og, od, wo, g2, wup, cw, cb, wdn):
    b, s, d = x.shape
    tm = TM_FFN
    nf = wup.shape[0]
    blk = lambda n: pl.BlockSpec((1, tm, n), lambda i, j: (i, j, 0))
    return pl.pallas_call(
        _ffn_kernel,
        grid=(b, s // tm),
        in_specs=[blk(d), blk(512), blk(512), _const_spec(wo.shape), _const_spec(g2.shape),
                  _const_spec(wup.shape), _const_spec(cw.shape), _const_spec(cb.shape),
                  _const_spec(wdn.shape)],
        out_specs=blk(d),
        out_shape=jax.ShapeDtypeStruct((b, s, d), F32),
        scratch_shapes=[pltpu.VMEM((nf, SUBLANES, 2 * FC), F32),
                        pltpu.VMEM((tm, d), BF16),
                        pltpu.VMEM((tm, d), F32),
                        pltpu.VMEM((tm + SUBLANES, 2 * FC), F32),
                        pltpu.VMEM((tm + SUBLANES, 2 * FC), F32)],
        compiler_params=pltpu.CompilerParams(dimension_semantics=("arbitrary", "arbitrary"),
                                             vmem_limit_bytes=VMEM_LIMIT),
        name="ffn",
    )(x, og, od, wo, g2, wup, cw, cb, wdn)


def _prep_layer(l, norm1_g, w_in, gla_w_gk_up, gla_b_gk, gla_norm_g, diff_q_norm_g, diff_k_norm_g,
                diff_lam_q1, diff_lam_k1, diff_lam_q2, diff_lam_k2, diff_sub_g, w_out, norm2_g,
                ffn_w_up, ffn_conv_w, ffn_conv_b, ffn_w_down):
    w = w_in[l]
    d = w.shape[0]
    low0 = 1536
    low = jnp.pad(w[:, low0:low0 + GLA_RANK], ((0, 0), (0, LANES - GLA_RANK)))
    w_r = jnp.concatenate([w[:, :low0], w[:, low0 + GLA_RANK:], low], axis=1).astype(BF16)
    wup_gate = jnp.pad(gla_w_gk_up[l], ((0, LANES - GLA_RANK), (0, 0))).astype(BF16)
    d_ff = ffn_w_down.shape[1]
    nf = d_ff // FC
    wu = ffn_w_up[l]
    wup = jnp.concatenate([wu[:, :d_ff].reshape(d, nf, FC), wu[:, d_ff:].reshape(d, nf, FC)],
                          axis=2).transpose(1, 0, 2).astype(BF16)
    cw = ffn_conv_w[l]
    cw_r = jnp.concatenate([cw[:, :d_ff].reshape(CONV_W, nf, FC), cw[:, d_ff:].reshape(CONV_W, nf, FC)],
                           axis=2).transpose(1, 0, 2)
    cb = ffn_conv_b[l]
    cb_r = jnp.concatenate([cb[:d_ff].reshape(nf, 1, FC), cb[d_ff:].reshape(nf, 1, FC)], axis=2)
    return dict(
        g1=norm1_g[l][None, :], w=w_r, wup_gate=wup_gate, bgk=gla_b_gk[l][None, :],
        qg=jnp.tile(diff_q_norm_g[l], 2 * DIFF_HEADS)[None, :],
        kg=jnp.tile(diff_k_norm_g[l], 2 * DIFF_HEADS)[None, :],
        ng=gla_norm_g[l][None, :],
        lamv=jnp.stack([diff_lam_q1[l], diff_lam_k1[l], diff_lam_q2[l], diff_lam_k2[l]]),
        fast=(1.02 * DIFF_DH ** 0.5 * jnp.max(jnp.abs(diff_q_norm_g[l])) * jnp.max(jnp.abs(diff_k_norm_g[l]))
              <= FAST_SCORE_BOUND).astype(jnp.int32).reshape(1),
        sg=diff_sub_g[l][None, :],
        wo=w_out[l].astype(BF16), g2=norm2_g[l][None, :],
        wup=wup, cw=cw_r, cb=cb_r, wdn=ffn_w_down[l].reshape(nf, FC, d).astype(BF16))


def kernel(x, norm1_g, w_in, gla_w_gk_up, gla_b_gk, gla_norm_g, diff_q_norm_g, diff_k_norm_g,
           diff_lam_q1, diff_lam_k1, diff_lam_q2, diff_lam_k2, diff_sub_g, w_out, norm2_g,
           ffn_w_up, ffn_conv_w, ffn_conv_b, ffn_w_down):
    b, s, d = x.shape
    depth = w_in.shape[0]
    assert s % TM_MIX == 0 and s % TQ == 0 and s % TM_FFN == 0
    assert ffn_w_down.shape[1] % FC == 0
    params = (norm1_g, w_in, gla_w_gk_up, gla_b_gk, gla_norm_g, diff_q_norm_g, diff_k_norm_g,
              diff_lam_q1, diff_lam_k1, diff_lam_q2, diff_lam_k2, diff_sub_g, w_out, norm2_g,
              ffn_w_up, ffn_conv_w, ffn_conv_b, ffn_w_down)
    for l in range(depth):
        p = _prep_layer(l, *params)
        lambda_init = 0.8 - 0.6 * math.exp(-0.3 * l)
        og, dq, dk, dv = _mix_in_call(x, p["g1"], p["w"], p["wup_gate"], p["bgk"], p["qg"], p["kg"], p["ng"])
        od = _attn_call(p["fast"], p["lamv"], p["sg"], dq, dk, dv, lambda_init)
        x = _ffn_call(x, og, od, p["wo"], p["g2"], p["wup"], p["cw"], p["cb"], p["wdn"])
    return x
```

```python
import functools
import math

import jax
import jax.numpy as jnp
from jax import lax
from jax.experimental import pallas as pl
from jax.experimental.pallas import tpu as pltpu

F32 = jnp.float32
BF16 = jnp.bfloat16

EPS = 1e-6
CHUNK = 64
GLA_HEADS = 4
GLA_DK = 64
GLA_DV = 128
GLA_RANK = 16
GLA_TAU = 16.0
DIFF_HEADS = 4
DIFF_DH = 64
DIFF_DV = 128
CONV_W = 3

LANES = 128
SUBLANES = 8
VMEM_LIMIT = 56 * 1024 * 1024

TM_MIX = 1024
TQ = 256
TM_FFN = 512
FC = 256
ATTN_PAIRS_PER_STEP = 8
PIECE = 256

NEG = -1e30
LOG2E = math.log2(math.e)
FAST_SCORE_BOUND = 40.0


def _dot(a, b):
    return jnp.dot(a, b, preferred_element_type=F32)


def _dot_nt(a, b):
    return lax.dot_general(a, b, (((1,), (1,)), ((), ())), preferred_element_type=F32)


def _dot_tn(a, b):
    return lax.dot_general(a, b, (((0,), (0,)), ((), ())), preferred_element_type=F32)


def _const_spec(shape):
    nd = len(shape)
    return pl.BlockSpec(shape, lambda *_: (0,) * nd, pipeline_mode=pl.Buffered(1))


def _mix_in_kernel(x_ref, g1_ref, w_ref, wup_ref, bgk_ref, qg_ref, kg_ref, ng_ref,
                   og_ref, dq_ref, dk_ref, dv_ref,
                   q_ref, k_ref, v_ref, r_ref, la_ref, st_ref, kvt_ref, stb_ref):
    @pl.when(pl.program_id(1) == 0)
    def _():
        st_ref[...] = jnp.zeros_like(st_ref)

    x = x_ref[0]
    ms = jnp.mean(x * x, axis=-1, keepdims=True)
    h = (x * lax.rsqrt(ms + EPS) * g1_ref[...]).astype(BF16)

    def qk_norm(p, g, scale):
        lo = lax.broadcasted_iota(jnp.int32, (p.shape[0], LANES), 1) < DIFF_DH
        blocks = []
        for j in range(p.shape[1] // LANES):
            blk = p[:, j * LANES:(j + 1) * LANES]
            sq = blk * blk
            s_lo = jnp.sum(jnp.where(lo, sq, 0.0), axis=-1, keepdims=True)
            s_hi = jnp.sum(jnp.where(lo, 0.0, sq), axis=-1, keepdims=True)
            blocks.append(blk * lax.rsqrt(jnp.where(lo, s_lo, s_hi) * (1.0 / DIFF_DH) + EPS))
        return (jnp.concatenate(blocks, axis=-1) * (g * scale)).astype(BF16)

    k_ref[...] = _dot(h, w_ref[:, 256:512]).astype(BF16)
    v_ref[...] = _dot(h, w_ref[:, 512:1024]).astype(BF16)

    low = _dot(h, w_ref[:, 3072:3200]).astype(BF16)
    z = _dot(low, wup_ref[...]) + bgk_ref[...]
    logsig = jnp.minimum(z, 0.0) - jnp.log1p(jnp.exp(-jnp.abs(z)))
    la_ref[...] = logsig * (1.0 / GLA_TAU)

    ri = lax.broadcasted_iota(jnp.int32, (CHUNK, CHUNK), 0)
    ci = lax.broadcasted_iota(jnp.int32, (CHUNK, CHUNK), 1)
    tril = jnp.where(ci <= ri, 1.0, 0.0).astype(BF16)
    lane = lax.broadcasted_iota(jnp.int32, (CHUNK, LANES), 1)
    head_mask = (lane < GLA_DK, lane >= GLA_DK)
    ng = ng_ref[...]
    n_chunks = TM_MIX // CHUNK
    n_pairs = GLA_HEADS // 2
    chunk_rows = lambda c: slice(c * CHUNK, (c + 1) * CHUNK)
    pair_lanes = lambda p: slice(p * LANES, (p + 1) * LANES)

    def piece(i):
        if i == 0:
            q_ref[...] = (_dot(h, w_ref[:, 0:PIECE]) * (GLA_DK ** -0.5)).astype(BF16)
            return
        seg, half = divmod(i - 1, 2)
        cols = slice(half * PIECE, (half + 1) * PIECE)
        w0 = 1024 + seg * 512 + half * PIECE
        p = _dot(h, w_ref[:, w0:w0 + PIECE])
        if seg == 0:
            r_ref[:, cols] = p.astype(BF16)
        elif seg == 1:
            dq_ref[0, :, cols] = qk_norm(p, qg_ref[:, cols], DIFF_DH ** -0.5 * LOG2E)
        elif seg == 2:
            dk_ref[0, :, cols] = qk_norm(p, kg_ref[:, cols], 1.0)
        else:
            dv_ref[0, :, cols] = p.astype(BF16)

    chunks = range(n_chunks)
    heads = [(pair, hh) for pair in range(n_pairs) for hh in range(2)]

    cums = []
    for c in chunks:
        la = la_ref[chunk_rows(c), :]
        la_hi = la.astype(BF16)
        la_lo = (la - la_hi.astype(F32)).astype(BF16)
        cums.append(_dot(tril, la_hi) + _dot(tril, la_lo))
    piece(0)
    piece(1)

    kdecs, decays = [], []
    for c in chunks:
        total = cums[c][CHUNK - 1:CHUNK, :]
        kdecs.append(k_ref[chunk_rows(c), :].astype(F32) * jnp.exp(total - cums[c]))
        decays.append(jnp.exp(total))
    piece(2)

    for c in chunks:
        for pair in range(n_pairs):
            kp = kdecs[c][:, pair_lanes(pair)]
            kvt = None
            for hh in range(2):
                head = 2 * pair + hh
                kh = jnp.where(head_mask[hh], kp, 0.0).astype(BF16)
                t = _dot_tn(v_ref[chunk_rows(c), head * GLA_DV:(head + 1) * GLA_DV], kh)
                kvt = t if kvt is None else kvt + t
            kvt_ref[c, pair] = kvt
    piece(3)
    piece(4)

    for pair in range(n_pairs):
        st = st_ref[pair]
        for c in chunks:
            st = st * decays[c][:, pair_lanes(pair)] + kvt_ref[c, pair]
            stb_ref[c, pair] = st.astype(BF16)
        st_ref[pair] = st
    piece(5)

    outs = {}
    for c in chunks:
        q = q_ref[chunk_rows(c), :]
        for pair, hh in heads:
            qp = q[:, pair_lanes(pair)]
            qh = jnp.where(head_mask[hh], qp, jnp.zeros_like(qp))
            outs[c, pair, hh] = _dot_nt(qh, stb_ref[c, pair])
    piece(6)
    piece(7)

    for c in chunks:
        for pair, hh in heads:
            cols = slice((2 * pair + hh) * GLA_DV, (2 * pair + hh + 1) * GLA_DV)
            o = outs[c, pair, hh]
            o = o * lax.rsqrt(jnp.mean(o * o, axis=-1, keepdims=True) + EPS) * ng
            r = r_ref[chunk_rows(c), cols].astype(F32)
            og_ref[0, chunk_rows(c), cols] = (o * (r * jax.nn.sigmoid(r))).astype(BF16)
    piece(8)


def _mix_in_call(x, g1, w, wup, bgk, qg, kg, ng):
    b, s, d = x.shape
    tm = TM_MIX
    blk = lambda n: pl.BlockSpec((1, tm, n), lambda i, j: (i, j, 0))
    n_chunks, n_pairs = tm // CHUNK, GLA_HEADS // 2
    return pl.pallas_call(
        _mix_in_kernel,
        grid=(b, s // tm),
        in_specs=[blk(d)] + [_const_spec(a.shape) for a in (g1, w, wup, bgk, qg, kg, ng)],
        out_specs=[blk(512)] * 4,
        out_shape=[jax.ShapeDtypeStruct((b, s, 512), BF16)] * 4,
        scratch_shapes=[pltpu.VMEM((tm, 256), BF16), pltpu.VMEM((tm, 256), BF16),
                        pltpu.VMEM((tm, 512), BF16), pltpu.VMEM((tm, 512), BF16),
                        pltpu.VMEM((tm, 256), F32),
                        pltpu.VMEM((n_pairs, GLA_DV, 2 * GLA_DK), F32),
                        pltpu.VMEM((n_chunks, n_pairs, GLA_DV, 2 * GLA_DK), F32),
                        pltpu.VMEM((n_chunks, n_pairs, GLA_DV, 2 * GLA_DK), BF16)],
        compiler_params=pltpu.CompilerParams(dimension_semantics=("arbitrary", "arbitrary"),
                                             vmem_limit_bytes=VMEM_LIMIT),
        name="mix_in",
    )(x, g1, w, wup, bgk, qg, kg, ng)


def _attn_kernel(flag_ref, lam_ref, sg_ref, q_ref, k_ref, v_ref, o_ref, *scratch, lambda_init):
    nq = q_ref.shape[1] // TQ

    def rows(tile):
        return pl.ds(pl.multiple_of(tile * TQ, TQ), TQ)

    def stacked_q(tile):
        q = q_ref[0, rows(tile), :]
        lane = lax.broadcasted_iota(jnp.int32, q.shape, 1)
        zero = jnp.zeros_like(q)
        return jnp.concatenate([jnp.where(lane < DIFF_DH, q, zero),
                                jnp.where(lane >= DIFF_DH, q, zero)], axis=0)

    def scores(qq, kv_tile, masked):
        s = _dot_nt(qq, k_ref[0, rows(kv_tile), :])
        if masked:
            rq = lax.broadcasted_iota(jnp.int32, s.shape, 0) % TQ
            ck = lax.broadcasted_iota(jnp.int32, s.shape, 1)
            s = jnp.where(ck // CHUNK <= rq // CHUNK, s, NEG)
        return s

    def finish(tile, acc, l):
        lamv = lam_ref[...]
        lam = (jnp.exp(jnp.sum(lamv[0:1] * lamv[1:2], axis=-1, keepdims=True))
               - jnp.exp(jnp.sum(lamv[2:3] * lamv[3:4], axis=-1, keepdims=True)) + lambda_init)
        o = acc / l
        o = o[:TQ] - lam * o[TQ:]
        o = o * lax.rsqrt(jnp.mean(o * o, axis=-1, keepdims=True) + EPS) * sg_ref[...]
        o_ref[0, rows(tile), :] = (o * (1.0 - lambda_init)).astype(BF16)

    @pl.when(flag_ref[0] == 1)
    def _():
        def pair(t, qq_ref, acc_ref, l_ref):
            def block(w, kv_tile, masked):
                p = jnp.exp2(scores(qq_ref[w], kv_tile, masked))
                l_ref[w] += sum(p[:, i * LANES:(i + 1) * LANES] for i in range(TQ // LANES))
                acc_ref[w] += _dot(p.astype(BF16), v_ref[0, rows(kv_tile), :])

            tiles = (t, nq - 1 - t)
            for w in range(2):
                qq_ref[w] = stacked_q(tiles[w])
            acc_ref[...] = jnp.zeros_like(acc_ref)
            l_ref[...] = jnp.zeros_like(l_ref)
            for j in range(nq - 1):
                w = jnp.where(j >= t, 1, 0)
                block(w, j - w * t, False)
            for w in range(2):
                block(w, tiles[w], True)
            for w in range(2):
                finish(tiles[w], acc_ref[w], jnp.sum(l_ref[w], axis=-1, keepdims=True))

        group = ATTN_PAIRS_PER_STEP if (nq // 2) % ATTN_PAIRS_PER_STEP == 0 else 1

        def body(i, _):
            for g in range(group):
                pair(i * group + g, *scratch[3 * g:3 * g + 3])
            return 0

        lax.fori_loop(0, nq // 2 // group, body, 0)

    @pl.when(flag_ref[0] == 0)
    def _():
        def tile_body(t, _):
            qq = stacked_q(t)

            def step(kv_tile, carry, masked):
                m, l, acc = carry
                s = scores(qq, kv_tile, masked)
                m_new = jnp.maximum(m, jnp.max(s, axis=-1, keepdims=True))
                p = jnp.exp2(s - m_new)
                alpha = jnp.exp2(m - m_new)
                l = alpha * l + jnp.sum(p, axis=-1, keepdims=True)
                acc = alpha * acc + _dot(p.astype(BF16), v_ref[0, rows(kv_tile), :])
                return m_new, l, acc

            init = (jnp.full((2 * TQ, 1), NEG, F32), jnp.zeros((2 * TQ, 1), F32),
                    jnp.zeros((2 * TQ, DIFF_DV), F32))
            carry = lax.fori_loop(0, t, lambda j, c: step(j, c, False), init)
            _, l, acc = step(t, carry, True)
            finish(t, acc, l)
            return 0

        lax.fori_loop(0, nq, tile_body, 0)


def _attn_call(flag, lamv, sg, dq, dk, dv, lambda_init):
    b, s, _ = dq.shape
    assert (s // TQ) % 2 == 0
    blk = pl.BlockSpec((1, s, LANES), lambda i, h, _: (i, 0, h))
    return pl.pallas_call(
        functools.partial(_attn_kernel, lambda_init=lambda_init),
        grid_spec=pltpu.PrefetchScalarGridSpec(
            num_scalar_prefetch=1,
            grid=(b, DIFF_HEADS),
            in_specs=[_const_spec(lamv.shape), _const_spec(sg.shape), blk, blk, blk],
            out_specs=blk,
            scratch_shapes=[pltpu.VMEM((2, 2 * TQ, LANES), BF16),
                            pltpu.VMEM((2, 2 * TQ, DIFF_DV), F32),
                            pltpu.VMEM((2, 2 * TQ, LANES), F32)] * ATTN_PAIRS_PER_STEP),
        out_shape=jax.ShapeDtypeStruct((b, s, DIFF_HEADS * DIFF_DV), BF16),
        compiler_params=pltpu.CompilerParams(
            dimension_semantics=("arbitrary", "arbitrary"),
            vmem_limit_bytes=VMEM_LIMIT),
        name="attn",
    )(flag, lamv, sg, dq, dk, dv)


def _ffn_kernel(x_ref, og_ref, od_ref, wo_ref, g2_ref, wup_ref, cw_ref, cb_ref, wdn_ref,
                out_ref, carry_ref, h2_ref, acc_ref, u0_ref, u1_ref):
    tm = TM_FFN
    nf = wup_ref.shape[0]
    assert nf % 2 == 1

    @pl.when(pl.program_id(1) == 0)
    def _():
        carry_ref[...] = jnp.zeros_like(carry_ref)

    half = wo_ref.shape[0] // 2
    x1 = x_ref[0] + _dot(og_ref[0], wo_ref[0:half, :]) + _dot(od_ref[0], wo_ref[half:, :])
    ms = jnp.mean(x1 * x1, axis=-1, keepdims=True)
    h2_ref[...] = (x1 * lax.rsqrt(ms + EPS) * g2_ref[...]).astype(BF16)
    acc_ref[...] = x1

    def up(f, u_ref):
        u = _dot(h2_ref[...], wup_ref[f])
        u_ref[0:SUBLANES, :] = carry_ref[f]
        u_ref[SUBLANES:SUBLANES + tm, :] = u
        carry_ref[f] = u[tm - SUBLANES:tm, :]

    def act(f, u_ref):
        cw = cw_ref[f]
        y = (cw[0:1] * u_ref[SUBLANES - 2:SUBLANES - 2 + tm, :]
             + cw[1:2] * u_ref[SUBLANES - 1:SUBLANES - 1 + tm, :]
             + cw[2:3] * u_ref[SUBLANES:SUBLANES + tm, :] + cb_ref[f])
        gate = y[:, :FC]
        return (gate * jax.nn.sigmoid(gate) * y[:, FC:]).astype(BF16)

    def down(f, u_ref):
        acc_ref[...] += _dot(act(f, u_ref), wdn_ref[f])

    up(0, u0_ref)

    def body(i, _):
        f = 2 * i
        up(f + 1, u1_ref)
        down(f, u0_ref)
        up(f + 2, u0_ref)
        down(f + 1, u1_ref)
        return 0

    lax.fori_loop(0, nf // 2, body, 0)
    down(nf - 1, u0_ref)
    out_ref[0] = acc_ref[...]


def _ffn_call(x, og, od, wo, g2, wup, cw, cb, wdn):
    b, s, d = x.shape
    tm = TM_FFN
    nf = wup.shape[0]
    blk = lambda n: pl.BlockSpec((1, tm, n), lambda i, j: (i, j, 0))
    return pl.pallas_call(
        _ffn_kernel,
        grid=(b, s // tm),
        in_specs=[blk(d), blk(512), blk(512), _const_spec(wo.shape), _const_spec(g2.shape),
                  _const_spec(wup.shape), _const_spec(cw.shape), _const_spec(cb.shape),
                  _const_spec(wdn.shape)],
        out_specs=blk(d),
        out_shape=jax.ShapeDtypeStruct((b, s, d), F32),
        scratch_shapes=[pltpu.VMEM((nf, SUBLANES, 2 * FC), F32),
                        pltpu.VMEM((tm, d), BF16),
                        pltpu.VMEM((tm, d), F32),
                        pltpu.VMEM((tm + SUBLANES, 2 * FC), F32),
                        pltpu.VMEM((tm + SUBLANES, 2 * FC), F32)],
        compiler_params=pltpu.CompilerParams(dimension_semantics=("arbitrary", "arbitrary"),
                                             vmem_limit_bytes=VMEM_LIMIT),
        name="ffn",
    )(x, og, od, wo, g2, wup, cw, cb, wdn)


def _prep_layer(l, norm1_g, w_in, gla_w_gk_up, gla_b_gk, gla_norm_g, diff_q_norm_g, diff_k_norm_g,
                diff_lam_q1, diff_lam_k1, diff_lam_q2, diff_lam_k2, diff_sub_g, w_out, norm2_g,
                ffn_w_up, ffn_conv_w, ffn_conv_b, ffn_w_down):
    w = w_in[l]
    d = w.shape[0]
    low0 = 1536
    low = jnp.pad(w[:, low0:low0 + GLA_RANK], ((0, 0), (0, LANES - GLA_RANK)))
    w_r = jnp.concatenate([w[:, :low0], w[:, low0 + GLA_RANK:], low], axis=1).astype(BF16)
    wup_gate = jnp.pad(gla_w_gk_up[l], ((0, LANES - GLA_RANK), (0, 0))).astype(BF16)
    d_ff = ffn_w_down.shape[1]
    nf = d_ff // FC
    wu = ffn_w_up[l]
    wup = jnp.concatenate([wu[:, :d_ff].reshape(d, nf, FC), wu[:, d_ff:].reshape(d, nf, FC)],
                          axis=2).transpose(1, 0, 2).astype(BF16)
    cw = ffn_conv_w[l]
    cw_r = jnp.concatenate([cw[:, :d_ff].reshape(CONV_W, nf, FC), cw[:, d_ff:].reshape(CONV_W, nf, FC)],
                           axis=2).transpose(1, 0, 2)
    cb = ffn_conv_b[l]
    cb_r = jnp.concatenate([cb[:d_ff].reshape(nf, 1, FC), cb[d_ff:].reshape(nf, 1, FC)], axis=2)
    return dict(
        g1=norm1_g[l][None, :], w=w_r, wup_gate=wup_gate, bgk=gla_b_gk[l][None, :],
        qg=jnp.tile(diff_q_norm_g[l], 2 * DIFF_HEADS)[None, :],
        kg=jnp.tile(diff_k_norm_g[l], 2 * DIFF_HEADS)[None, :],
        ng=gla_norm_g[l][None, :],
        lamv=jnp.stack([diff_lam_q1[l], diff_lam_k1[l], diff_lam_q2[l], diff_lam_k2[l]]),
        fast=(1.02 * DIFF_DH ** 0.5 * jnp.max(jnp.abs(diff_q_norm_g[l])) * jnp.max(jnp.abs(diff_k_norm_g[l]))
              <= FAST_SCORE_BOUND).astype(jnp.int32).reshape(1),
        sg=diff_sub_g[l][None, :],
        wo=w_out[l].astype(BF16), g2=norm2_g[l][None, :],
        wup=wup, cw=cw_r, cb=cb_r, wdn=ffn_w_down[l].reshape(nf, FC, d).astype(BF16))


def kernel(x, norm1_g, w_in, gla_w_gk_up, gla_b_gk, gla_norm_g, diff_q_norm_g, diff_k_norm_g,
           diff_lam_q1, diff_lam_k1, diff_lam_q2, diff_lam_k2, diff_sub_g, w_out, norm2_g,
           ffn_w_up, ffn_conv_w, ffn_conv_b, ffn_w_down):
    b, s, d = x.shape
    depth = w_in.shape[0]
    assert s % TM_MIX == 0 and s % TQ == 0 and s % TM_FFN == 0
    assert ffn_w_down.shape[1] % FC == 0
    params = (norm1_g, w_in, gla_w_gk_up, gla_b_gk, gla_norm_g, diff_q_norm_g, diff_k_norm_g,
              diff_lam_q1, diff_lam_k1, diff_lam_q2, diff_lam_k2, diff_sub_g, w_out, norm2_g,
              ffn_w_up, ffn_conv_w, ffn_conv_b, ffn_w_down)
    for l in range(depth):
        p = _prep_layer(l, *params)
        lambda_init = 0.8 - 0.6 * math.exp(-0.3 * l)
        og, dq, dk, dv = _mix_in_call(x, p["g1"], p["w"], p["wup_gate"], p["bgk"], p["qg"], p["kg"], p["ng"])
        od = _attn_call(p["fast"], p["lamv"], p["sg"], dq, dk, dv, lambda_init)
        x = _ffn_call(x, og, od, p["wo"], p["g2"], p["wup"], p["cw"], p["cb"], p["wdn"])
    return x
```
